```python
import jax, jax.numpy as jnp
from jax import lax
import numpy as np

D_MODEL = 1024
BATCH = 2
SEQ = 8192
DEPTH = 4
DEC_BATCH = 32
DEC_SEQ = 1
PAST_LEN = 8192
PAGE_SIZE = 128

N_MIXERS = 3
N_POOL_LAYERS = (DEPTH + 2) // 3
N_DIL_LAYERS = (DEPTH + 1) // 3
N_MOBA_LAYERS = DEPTH // 3

POOL_WINDOWS = (2, 4, 8, 16)
POOL_GROUPS = len(POOL_WINDOWS)
POOL_GROUP_DIM = D_MODEL // POOL_GROUPS
POOL_HIST = max(POOL_WINDOWS) - 1

DIL_PATTERNS = ((128, 1), (512, 4), (2048, 16))
N_DIL_GROUPS = len(DIL_PATTERNS)
DIL_BAND = 128
DIL_HEADS = 8
DIL_HEAD_DIM = D_MODEL // DIL_HEADS

MOBA_BLOCK = 256
MOBA_TOPK = 3
MOBA_HEADS = 16
MOBA_HEAD_DIM = D_MODEL // MOBA_HEADS
MOBA_QCHUNK = 32

FFN_HIDDEN = ((8 * D_MODEL // 3 + 127) // 128) * 128
CONV_WIDTH = 3
CONV_HIST = CONV_WIDTH - 1
RMS_EPS = 1e-6

kernel_name = 'hybrid_pool_dilated_moba_convffn_step'


def rms_norm(x, g):
    xf = x.astype(jnp.float32)
    y = xf * lax.rsqrt(jnp.mean(xf * xf, axis=-1, keepdims=True) + RMS_EPS)
    return (y * g.astype(jnp.float32)).astype(x.dtype)


def alibi_slopes(n):
    return jnp.exp2(-8.0 * jnp.arange(1, n + 1, dtype=jnp.float32) / n)


def pool_mix(h_ext, n_out, w, scale):
    B, L, D = h_ext.shape
    hf = h_ext.astype(jnp.float32)
    cs = jnp.concatenate([jnp.zeros((B, 1, D), jnp.float32), jnp.cumsum(hf, axis=1)], axis=1)
    rows = jnp.arange(L - n_out, L)
    upper = cs[:, rows + 1]
    cur = hf[:, L - n_out:]
    diffs = []
    for gi, win in enumerate(POOL_WINDOWS):
        c0, c1 = gi * POOL_GROUP_DIM, (gi + 1) * POOL_GROUP_DIM
        lo = jnp.maximum(rows + 1 - win, 0)
        cnt = jnp.minimum(rows + 1, win).astype(jnp.float32)[None, :, None]
        mean = (upper[..., c0:c1] - cs[:, lo, c0:c1]) / cnt
        diffs.append(mean - cur[..., c0:c1])
    d = jnp.stack(diffs, axis=2).astype(h_ext.dtype)
    y = jnp.einsum('btgc,gce->btge', d, w).reshape(B, n_out, D)
    return y * scale


def dil_project(h, w_qkv, q_gain, k_gain):
    B, T, _ = h.shape
    qkv = (h @ w_qkv).reshape(B, T, N_DIL_GROUPS, 3, DIL_HEADS, DIL_HEAD_DIM)
    q = rms_norm(qkv[:, :, :, 0], q_gain[None, None, :, None, :])
    k = rms_norm(qkv[:, :, :, 1], k_gain[None, None, :, None, :])
    return q, k, qkv[:, :, :, 2]


def dilated_prompt(q, k, v, slopes, dil):
    B, S, H, E = q.shape
    n_sub = S // dil
    n_blk = -(-n_sub // DIL_BAND)
    pad = n_blk * DIL_BAND - n_sub

    def strided(x):
        x = x.reshape(B, n_sub, dil, H, E).transpose(0, 2, 1, 3, 4)
        x = jnp.pad(x, ((0, 0), (0, 0), (0, pad), (0, 0), (0, 0)))
        return x.reshape(B, dil, n_blk, DIL_BAND, H, E)

    def banded(x):
        prev = jnp.pad(x[:, :, :-1], ((0, 0), (0, 0), (1, 0), (0, 0), (0, 0), (0, 0)))
        return jnp.concatenate([prev, x], axis=3)

    qs = strided(q)
    kb = banded(strided(k))
    vb = banded(strided(v))
    s = jnp.einsum('brnqhe,brnkhe->brnhqk', qs, kb, preferred_element_type=jnp.float32) * (E ** -0.5)
    qi = jnp.arange(DIL_BAND)[:, None]
    kj = jnp.arange(2 * DIL_BAND)[None, :]
    dsub = qi - kj + DIL_BAND
    blk = jnp.arange(n_blk)[:, None, None]
    valid = (dsub >= 0) & (dsub <= DIL_BAND) & (blk * DIL_BAND - DIL_BAND + kj >= 0)
    s = s - slopes[:, None, None] * (dsub * dil).astype(jnp.float32)
    s = jnp.where(valid[:, None], s, -jnp.inf)
    lse = jax.nn.logsumexp(s, axis=-1)
    p = jnp.exp(s - lse[..., None]).astype(v.dtype)
    o = jnp.einsum('brnhqk,brnkhe->brnqhe', p, vb, preferred_element_type=jnp.float32)
    o = o.reshape(B, dil, n_blk * DIL_BAND, H, E)[:, :, :n_sub].transpose(0, 2, 1, 3, 4).reshape(B, S, H, E)
    lse = lse.transpose(0, 1, 2, 4, 3).reshape(B, dil, n_blk * DIL_BAND, H)[:, :, :n_sub]
    lse = lse.transpose(0, 2, 1, 3).reshape(B, S, H)
    return o, lse


def dilated_decode(q, k_ext, v_ext, slopes, dil, q_pos, pos0):
    L = k_ext.shape[1]
    E = q.shape[-1]
    offs = jnp.arange(DIL_BAND + 1) * dil
    kpos = q_pos[:, None] - offs[None, :]
    valid = kpos >= 0
    idx = jnp.clip(kpos - pos0, 0, L - 1)
    kg = k_ext[:, idx]
    vg = v_ext[:, idx]
    s = jnp.einsum('bthe,btjhe->bhtj', q, kg, preferred_element_type=jnp.float32) * (E ** -0.5)
    s = s - slopes[None, :, None, None] * offs.astype(jnp.float32)[None, None, None, :]
    s = jnp.where(valid[None, None], s, -jnp.inf)
    lse = jax.nn.logsumexp(s, axis=-1)
    p = jnp.exp(s - lse[..., None]).astype(vg.dtype)
    o = jnp.einsum('bhtj,btjhe->bthe', p, vg, preferred_element_type=jnp.float32)
    return o, lse.transpose(0, 2, 1)


def merge_groups(outs, lses):
    w = jax.nn.softmax(jnp.stack(lses, axis=0), axis=0)
    o = jnp.sum(w[..., None] * jnp.stack(outs, axis=0), axis=0)
    return o.reshape(o.shape[0], o.shape[1], -1)


def dilated_layer(hp, hs, bufs, pos_s, w_qkv, q_gain, k_gain, w_o, slopes):
    qp, kp, vp = dil_project(hp, w_qkv, q_gain, k_gain)
    qs, ks, vs = dil_project(hs, w_qkv, q_gain, k_gain)
    S = hp.shape[1]
    outs_p, lses_p, outs_s, lses_s, rows_p, rows_s = [], [], [], [], [], []
    for g, (win, dil) in enumerate(DIL_PATTERNS):
        o, l = dilated_prompt(qp[:, :, g], kp[:, :, g], vp[:, :, g], slopes[g], dil)
        outs_p.append(o)
        lses_p.append(l)
        buf = bufs[g].astype(ks.dtype)
        wb = buf.shape[1]
        k_ext = jnp.concatenate([buf[:, :, 0], ks[:, :, g]], axis=1)
        v_ext = jnp.concatenate([buf[:, :, 1], vs[:, :, g]], axis=1)
        o2, l2 = dilated_decode(qs[:, :, g], k_ext, v_ext, slopes[g], dil, pos_s, PAST_LEN - wb)
        outs_s.append(o2)
        lses_s.append(l2)
        rows_p.append(jnp.stack([kp[:, :, g], vp[:, :, g]], axis=2)[:, S - min(win, S):])
        rows_s.append(jnp.stack([ks[:, :, g], vs[:, :, g]], axis=2))
    mp = merge_groups(outs_p, lses_p).astype(hp.dtype) @ w_o
    ms = merge_groups(outs_s, lses_s).astype(hs.dtype) @ w_o
    return mp, ms, rows_p, rows_s


def moba_project(h, w_qkv, q_gain, k_gain):
    B, T, _ = h.shape
    qkv = (h @ w_qkv).reshape(B, T, 3, MOBA_HEADS, MOBA_HEAD_DIM)
    return rms_norm(qkv[:, :, 0], q_gain), rms_norm(qkv[:, :, 1], k_gain), qkv[:, :, 2]


def moba_blocks(kv):
    B, L = kv.shape[:2]
    kvb = kv.reshape(B, L // MOBA_BLOCK, MOBA_BLOCK, 2, MOBA_HEADS, MOBA_HEAD_DIM)
    k_mean = jnp.mean(kvb[:, :, :, 0].astype(jnp.float32), axis=2)
    return kvb, k_mean


def moba_attend(q, q_pos, kvb, k_mean, slopes):
    B, Tq, H, E = q.shape
    NB = kvb.shape[1]
    n_sel = min(MOBA_TOPK, NB)
    own = q_pos // MOBA_BLOCK
    gate = jnp.einsum('bthe,bnhe->bhtn', q.astype(jnp.float32), k_mean)
    is_past = jnp.arange(NB)[None, :] < own[:, None]
    gate = jnp.where(is_past[None, None], gate, -jnp.inf)
    _, sel = lax.top_k(gate, n_sel)
    own_b = jnp.broadcast_to(own[None, None, :, None], (B, H, Tq, 1)).astype(sel.dtype)
    blocks = jnp.concatenate([sel, own_b], axis=-1)
    bi = jnp.arange(B)[:, None, None, None]
    hi = jnp.arange(H)[None, :, None, None]
    kvg = kvb[bi, blocks, :, :, hi]
    kpos = blocks[..., None] * MOBA_BLOCK + jnp.arange(MOBA_BLOCK)
    qp = q_pos[None, None, :, None, None]
    slot_own = (jnp.arange(n_sel + 1) == n_sel)[:, None]
    valid = jnp.where(slot_own, kpos <= qp, blocks[..., None] < own[None, None, :, None, None])
    s = jnp.einsum('bthe,bhtjce->bhtjc', q, kvg[..., 0, :], preferred_element_type=jnp.float32) * (E ** -0.5)
    s = s - slopes[None, :, None, None, None] * (qp - kpos).astype(jnp.float32)
    s = jnp.where(valid, s, -jnp.inf)
    p = jax.nn.softmax(s.reshape(B, H, Tq, -1), axis=-1).reshape(s.shape).astype(kvg.dtype)
    return jnp.einsum('bhtjc,bhtjce->bthe', p, kvg[..., 1, :], preferred_element_type=jnp.float32)


def moba_layer(hp, hs, cache, page_table, pos_s, w_qkv, q_gain, k_gain, w_o, slopes):
    qp, kp, vp = moba_project(hp, w_qkv, q_gain, k_gain)
    qs, ks, vs = moba_project(hs, w_qkv, q_gain, k_gain)
    B, S = hp.shape[:2]
    kv_p = jnp.stack([kp, vp], axis=2)
    nb_p = -(-S // MOBA_BLOCK)
    kvb_p, km_p = moba_blocks(jnp.pad(kv_p, ((0, 0), (0, nb_p * MOBA_BLOCK - S), (0, 0), (0, 0), (0, 0))))
    n_chunk = S // MOBA_QCHUNK
    qc = qp.reshape(B, n_chunk, MOBA_QCHUNK, MOBA_HEADS, MOBA_HEAD_DIM).transpose(1, 0, 2, 3, 4)
    pc = jnp.arange(S, dtype=jnp.int32).reshape(n_chunk, MOBA_QCHUNK)
    op = lax.map(lambda a: moba_attend(a[0], a[1], kvb_p, km_p, slopes), (qc, pc))
    op = op.transpose(1, 0, 2, 3, 4).reshape(B, S, -1)
    DB, T = hs.shape[:2]
    kv_s = jnp.stack([ks, vs], axis=2)
    past = cache[page_table].reshape(DB, -1, 2, MOBA_HEADS, MOBA_HEAD_DIM).astype(kv_s.dtype)
    L = past.shape[1] + T
    nb_s = -(-L // MOBA_BLOCK)
    zpad = jnp.zeros((DB, nb_s * MOBA_BLOCK - L) + kv_s.shape[2:], kv_s.dtype)
    kvb_s, km_s = moba_blocks(jnp.concatenate([past, kv_s, zpad], axis=1))
    os_ = moba_attend(qs, pos_s, kvb_s, km_s, slopes).reshape(DB, T, -1)
    mp = op.astype(hp.dtype) @ w_o
    ms = os_.astype(hs.dtype) @ w_o
    return mp, ms, kv_p, kv_s


def conv_ffn(h, hist, w_in, conv_w, conv_b, w_out):
    T = h.shape[1]
    u = h @ w_in
    ue = jnp.concatenate([hist.astype(u.dtype), u], axis=1)
    c = conv_b
    for tap in range(CONV_WIDTH):
        c = c + conv_w[tap] * ue[:, tap:tap + T]
    a, g = jnp.split(c, 2, axis=-1)
    y = (jax.nn.silu(g) * a) @ w_out
    return y, ue[:, -CONV_HIST:]


def setup_inputs(seed: int = 0) -> dict:
    key = jax.random.key(seed)
    k = jax.random.split(key, 25)
    f32 = jnp.float32

    def normal(kk, shape, scale=1.0):
        return jax.random.normal(kk, shape, f32) * scale

    def gain(kk, shape):
        return 1.0 + 0.02 * jax.random.normal(kk, shape, f32)

    n_pages = PAST_LEN // PAGE_SIZE
    n_used = DEC_BATCH * n_pages
    n_phys = n_used + n_used // 4
    page_table = jax.random.permutation(k[0], n_phys)[:n_used].reshape(DEC_BATCH, n_pages).astype(jnp.int32)
    dil_rows = [min(w, PAST_LEN) for w, _ in DIL_PATTERNS]
    dil_cols = N_DIL_GROUPS * 3 * DIL_HEADS * DIL_HEAD_DIM
    dil_out = DIL_HEADS * DIL_HEAD_DIM
    moba_out = MOBA_HEADS * MOBA_HEAD_DIM
    f2 = 2 * FFN_HIDDEN
    return {
        'x_prompt': normal(k[1], (BATCH, SEQ, D_MODEL)),
        'x_sample': normal(k[2], (DEC_BATCH, DEC_SEQ, D_MODEL)),
        'state_pool': normal(k[3], (N_POOL_LAYERS, DEC_BATCH, POOL_HIST, D_MODEL)),
        'cache_dil0': normal(k[4], (N_DIL_LAYERS, DEC_BATCH, dil_rows[0], 2, DIL_HEADS, DIL_HEAD_DIM)),
        'cache_dil1': normal(k[5], (N_DIL_LAYERS, DEC_BATCH, dil_rows[1], 2, DIL_HEADS, DIL_HEAD_DIM)),
        'cache_dil2': normal(k[6], (N_DIL_LAYERS, DEC_BATCH, dil_rows[2], 2, DIL_HEADS, DIL_HEAD_DIM)),
        'state_conv': normal(k[7], (DEPTH, DEC_BATCH, CONV_HIST, f2)),
        'cache_moba': normal(k[8], (N_MOBA_LAYERS, n_phys, PAGE_SIZE, 2, MOBA_HEADS, MOBA_HEAD_DIM)),
        'page_table': page_table,
        'norm_mix': gain(k[9], (DEPTH, D_MODEL)),
        'norm_ffn': gain(k[10], (DEPTH, D_MODEL)),
        'pool_w': normal(k[11], (N_POOL_LAYERS, POOL_GROUPS, POOL_GROUP_DIM, POOL_GROUP_DIM), POOL_GROUP_DIM ** -0.5),
        'pool_scale': gain(k[12], (N_POOL_LAYERS, D_MODEL)),
        'dil_w_qkv': normal(k[13], (N_DIL_LAYERS, D_MODEL, dil_cols), D_MODEL ** -0.5),
        'dil_q_gain': gain(k[14], (N_DIL_LAYERS, N_DIL_GROUPS, DIL_HEAD_DIM)),
        'dil_k_gain': gain(k[15], (N_DIL_LAYERS, N_DIL_GROUPS, DIL_HEAD_DIM)),
        'dil_w_o': normal(k[16], (N_DIL_LAYERS, dil_out, D_MODEL), dil_out ** -0.5),
        'moba_w_qkv': normal(k[17], (N_MOBA_LAYERS, D_MODEL, 3 * moba_out), D_MODEL ** -0.5),
        'moba_q_gain': gain(k[18], (N_MOBA_LAYERS, MOBA_HEAD_DIM)),
        'moba_k_gain': gain(k[19], (N_MOBA_LAYERS, MOBA_HEAD_DIM)),
        'moba_w_o': normal(k[20], (N_MOBA_LAYERS, moba_out, D_MODEL), moba_out ** -0.5),
        'ffn_w_in': normal(k[21], (DEPTH, D_MODEL, f2), D_MODEL ** -0.5),
        'ffn_conv_w': normal(k[22], (DEPTH, CONV_WIDTH, f2), CONV_WIDTH ** -0.5),
        'ffn_conv_b': normal(k[23], (DEPTH, f2), 0.02),
        'ffn_w_out': normal(k[24], (DEPTH, FFN_HIDDEN, D_MODEL), FFN_HIDDEN ** -0.5),
    }


def reference(x_prompt, x_sample, state_pool, cache_dil0, cache_dil1, cache_dil2, state_conv, cache_moba,
              page_table, norm_mix, norm_ffn, pool_w, pool_scale, dil_w_qkv, dil_q_gain, dil_k_gain, dil_w_o,
              moba_w_qkv, moba_q_gain, moba_k_gain, moba_w_o, ffn_w_in, ffn_conv_w, ffn_conv_b, ffn_w_out):
    cache_dil = (cache_dil0, cache_dil1, cache_dil2)
    n_new = x_sample.shape[1]
    pos_s = PAST_LEN + jnp.arange(n_new, dtype=jnp.int32)
    dil_slopes = alibi_slopes(N_DIL_GROUPS * DIL_HEADS).reshape(N_DIL_GROUPS, DIL_HEADS)
    moba_slopes = alibi_slopes(MOBA_HEADS)
    pool_p, pool_s, moba_p, moba_s, conv_p, conv_s = [], [], [], [], [], []
    dil_p = [[] for _ in range(N_DIL_GROUPS)]
    dil_s = [[] for _ in range(N_DIL_GROUPS)]
    yp, ys = x_prompt, x_sample
    for layer in range(DEPTH):
        kind = layer % N_MIXERS
        j = layer // N_MIXERS
        hp = rms_norm(yp, norm_mix[layer])
        hs = rms_norm(ys, norm_mix[layer])
        if kind == 0:
            mp = pool_mix(hp, hp.shape[1], pool_w[j], pool_scale[j])
            ext = jnp.concatenate([state_pool[j].astype(hs.dtype), hs], axis=1)
            ms = pool_mix(ext, n_new, pool_w[j], pool_scale[j])
            pool_p.append(hp[:, -POOL_HIST:])
            pool_s.append(ext[:, -POOL_HIST:])
        elif kind == 1:
            mp, ms, rows_p, rows_s = dilated_layer(hp, hs, [c[j] for c in cache_dil], pos_s, dil_w_qkv[j],
                                                   dil_q_gain[j], dil_k_gain[j], dil_w_o[j], dil_slopes)
            for g in range(N_DIL_GROUPS):
                dil_p[g].append(rows_p[g])
                dil_s[g].append(rows_s[g])
        else:
            mp, ms, rows_p, rows_s = moba_layer(hp, hs, cache_moba[j], page_table, pos_s, moba_w_qkv[j],
                                                moba_q_gain[j], moba_k_gain[j], moba_w_o[j], moba_slopes)
            moba_p.append(rows_p)
            moba_s.append(rows_s)
        yp = yp + mp.astype(yp.dtype)
        ys = ys + ms.astype(ys.dtype)
        hp = rms_norm(yp, norm_ffn[layer])
        hs = rms_norm(ys, norm_ffn[layer])
        zero_hist = jnp.zeros((hp.shape[0], CONV_HIST, ffn_w_in.shape[-1]), hp.dtype)
        fp, cp = conv_ffn(hp, zero_hist, ffn_w_in[layer], ffn_conv_w[layer], ffn_conv_b[layer], ffn_w_out[layer])
        fs, cs = conv_ffn(hs, state_conv[layer], ffn_w_in[layer], ffn_conv_w[layer], ffn_conv_b[layer], ffn_w_out[layer])
        yp = yp + fp.astype(yp.dtype)
        ys = ys + fs.astype(ys.dtype)
        conv_p.append(cp)
        conv_s.append(cs)
    y_prompt, y_sample = yp, ys
    new_pool_prompt = jnp.stack(pool_p, 0)
    new_pool_sample = jnp.stack(pool_s, 0)
    new_dil0_prompt = jnp.stack(dil_p[0], 0)
    new_dil1_prompt = jnp.stack(dil_p[1], 0)
    new_dil2_prompt = jnp.stack(dil_p[2], 0)
    new_dil0_sample = jnp.stack(dil_s[0], 0)
    new_dil1_sample = jnp.stack(dil_s[1], 0)
    new_dil2_sample = jnp.stack(dil_s[2], 0)
    new_moba_prompt = jnp.stack(moba_p, 0)
    new_moba_sample = jnp.stack(moba_s, 0)
    new_conv_prompt = jnp.stack(conv_p, 0)
    new_conv_sample = jnp.stack(conv_s, 0)
    return (y_prompt, y_sample, new_pool_prompt, new_pool_sample, new_dil0_prompt, new_dil1_prompt,
            new_dil2_prompt, new_dil0_sample, new_dil1_sample, new_dil2_sample, new_moba_prompt,
            new_moba_sample, new_conv_prompt, new_conv_sample)
```

```python
import functools

import jax
import jax.numpy as jnp
from jax import lax
from jax.experimental import pallas as pl
from jax.experimental.pallas import tpu as pltpu

F32 = jnp.float32
BF16 = jnp.bfloat16

N_MIXERS = 3
POOL_WINDOWS = (2, 4, 8, 16)
POOL_HIST = max(POOL_WINDOWS) - 1
POOL_CARRY = 16
DIL_PATTERNS = ((128, 1), (512, 4), (2048, 16))
DIL_BAND = 128
DIL_HEADS = 8
MOBA_BLOCK = 256
MOBA_TOPK = 3
MOBA_HEADS = 16
PAGE_SIZE = 128
CONV_WIDTH = 3
CONV_HIST = CONV_WIDTH - 1
CONV_CARRY = 8
RMS_EPS = 1e-6
NEG_INF = float("-inf")
VMEM_LIMIT = 56 * 1024 * 1024


def _params(*sem):
    return pltpu.CompilerParams(dimension_semantics=sem, vmem_limit_bytes=VMEM_LIMIT)


def _resident(shape):
    zeros = (0,) * len(shape)
    return pl.BlockSpec(shape, lambda *_: zeros, pipeline_mode=pl.Buffered(1))


def _rms(x, g):
    return x * lax.rsqrt(jnp.mean(x * x, axis=-1, keepdims=True) + RMS_EPS) * g


def _dot(a, b):
    return jnp.dot(a, b, preferred_element_type=F32)


def _dot_nt(a, b, precision=None):
    return lax.dot_general(a, b, (((1,), (1,)), ((), ())), preferred_element_type=F32, precision=precision)


def _alibi_slopes(n):
    return [2.0 ** (-8.0 * i / n) for i in range(1, n + 1)]


def _pool_prompt_kernel(x_ref, g_ref, w_ref, sc_ref, y_ref, tail_ref, ext_ref, *, tile):
    t = pl.program_id(1)
    d_model = x_ref.shape[-1]
    gdim = d_model // len(POOL_WINDOWS)

    @pl.when(t == 0)
    def _():
        ext_ref[0:POOL_CARRY, :] = jnp.zeros((POOL_CARRY, d_model), F32)

    x = x_ref[0]
    ext_ref[POOL_CARRY:POOL_CARRY + tile, :] = _rms(x, g_ref[...])
    row = t * tile + lax.broadcasted_iota(jnp.int32, (tile, 1), 0)
    for gi, win in enumerate(POOL_WINDOWS):
        cols = slice(gi * gdim, (gi + 1) * gdim)
        cur = ext_ref[POOL_CARRY:POOL_CARRY + tile, cols]
        acc = cur
        for j in range(1, win):
            acc = acc + ext_ref[POOL_CARRY - j:POOL_CARRY - j + tile, cols]
        cnt = jnp.minimum(row + 1, win).astype(F32)
        d = acc / cnt - cur
        y = _dot(d.astype(BF16), w_ref[gi])
        y_ref[0, :, cols] = x[:, cols] + y * sc_ref[:, cols]
    last = ext_ref[tile:tile + POOL_CARRY, :]
    ext_ref[0:POOL_CARRY, :] = last

    @pl.when(t == pl.num_programs(1) - 1)
    def _():
        tail_ref[0] = last


def _pool_prompt(x, g, w16, sc, tile=512):
    b, s, d = x.shape
    tile = min(tile, s)
    return pl.pallas_call(
        functools.partial(_pool_prompt_kernel, tile=tile),
        grid=(b, s // tile),
        in_specs=[pl.BlockSpec((1, tile, d), lambda i, t: (i, t, 0)),
                  _resident((1, d)), _resident(w16.shape), _resident((1, d))],
        out_specs=[pl.BlockSpec((1, tile, d), lambda i, t: (i, t, 0)),
                   pl.BlockSpec((1, POOL_CARRY, d), lambda i, t: (i, 0, 0))],
        out_shape=[jax.ShapeDtypeStruct((b, s, d), F32), jax.ShapeDtypeStruct((b, POOL_CARRY, d), F32)],
        scratch_shapes=[pltpu.VMEM((POOL_CARRY + tile, d), F32)],
        compiler_params=_params("arbitrary", "arbitrary"),
        name="pool_prompt",
    )(x, g, w16, sc)


def _pool_sample_kernel(x_ref, st_ref, g_ref, w_ref, sc_ref, y_ref, h_ref):
    d_model = x_ref.shape[-1]
    gdim = d_model // len(POOL_WINDOWS)
    x = x_ref[...]
    h = _rms(x, g_ref[...])
    h_ref[...] = h
    for gi, win in enumerate(POOL_WINDOWS):
        cols = slice(gi * gdim, (gi + 1) * gdim)
        cur = h[:, cols]
        acc = cur
        for j in range(1, win):
            acc = acc + st_ref[POOL_HIST - j, :, cols]
        d = acc / float(win) - cur
        y = _dot(d.astype(BF16), w_ref[gi])
        y_ref[:, cols] = x[:, cols] + y * sc_ref[:, cols]


def _pool_sample(x, state_t, g, w16, sc):
    n, d = x.shape
    return pl.pallas_call(
        _pool_sample_kernel,
        out_shape=[jax.ShapeDtypeStruct((n, d), F32), jax.ShapeDtypeStruct((n, d), F32)],
        compiler_params=pltpu.CompilerParams(vmem_limit_bytes=VMEM_LIMIT),
        name="pool_sample",
    )(x, state_t, g, w16, sc)


def _silu(g):
    return g * (1.0 / (1.0 + jnp.exp(-g)))


def _ffn_prompt_kernel(y_ref, g_ref, win_ref, cw_ref, cb_ref, wout_ref, o_ref, tail_ref,
                       u_ref, carry_ref, acc_ref, *, tile, chunk):
    t = pl.program_id(1)
    hidden = wout_ref.shape[0]

    @pl.when(t == 0)
    def _():
        carry_ref[...] = jnp.zeros(carry_ref.shape, F32)

    y = y_ref[0]
    h = _rms(y, g_ref[...]).astype(BF16)
    for c in range(hidden // chunk):
        conv = []
        for part in range(2):
            cols = slice(part * hidden + c * chunk, part * hidden + (c + 1) * chunk)
            u = _dot(h, win_ref[:, cols])
            u_ref[part, 0:CONV_CARRY, :] = carry_ref[:, cols]
            u_ref[part, CONV_CARRY:CONV_CARRY + tile, :] = u
            carry_ref[:, cols] = u[tile - CONV_CARRY:tile, :]
            cv = cb_ref[:, cols] + cw_ref[2:3, cols] * u
            cv = cv + cw_ref[1:2, cols] * u_ref[part, CONV_CARRY - 1:CONV_CARRY - 1 + tile, :]
            cv = cv + cw_ref[0:1, cols] * u_ref[part, CONV_CARRY - 2:CONV_CARRY - 2 + tile, :]
            conv.append(cv)
        act = (_silu(conv[1]) * conv[0]).astype(BF16)
        contrib = _dot(act, wout_ref[c * chunk:(c + 1) * chunk, :])
        if c == 0:
            acc_ref[...] = contrib
        else:
            acc_ref[...] += contrib
    o_ref[0] = y + acc_ref[...]

    @pl.when(t == pl.num_programs(1) - 1)
    def _():
        tail_ref[0] = carry_ref[...]


def _ffn_prompt(y, g, win16, cw, cb, wout16, tile=512, chunk=256):
    b, s, d = y.shape
    hidden = wout16.shape[0]
    tile = min(tile, s)
    return pl.pallas_call(
        functools.partial(_ffn_prompt_kernel, tile=tile, chunk=chunk),
        grid=(b, s // tile),
        in_specs=[pl.BlockSpec((1, tile, d), lambda i, t: (i, t, 0)),
                  _resident((1, d)), _resident(win16.shape), _resident(cw.shape), _resident(cb.shape),
                  _resident(wout16.shape)],
        out_specs=[pl.BlockSpec((1, tile, d), lambda i, t: (i, t, 0)),
                   pl.BlockSpec((1, CONV_CARRY, 2 * hidden), lambda i, t: (i, 0, 0))],
        out_shape=[jax.ShapeDtypeStruct((b, s, d), F32),
                   jax.ShapeDtypeStruct((b, CONV_CARRY, 2 * hidden), F32)],
        scratch_shapes=[pltpu.VMEM((2, CONV_CARRY + tile, chunk), F32),
                        pltpu.VMEM((CONV_CARRY, 2 * hidden), F32),
                        pltpu.VMEM((tile, d), F32)],
        compiler_params=_params("arbitrary", "arbitrary"),
        name="ffn_prompt",
    )(y, g, win16, cw, cb, wout16)


def _ffn_sample_kernel(y_ref, g_ref, wa_ref, wg_ref, ha_ref, hg_ref, cwa_ref, cwg_ref, cba_ref, cbg_ref,
                       wout_ref, o_ref, ua_ref, ug_ref):
    c = pl.program_id(0)
    y = y_ref[...]
    h = _rms(y, g_ref[...]).astype(BF16)

    def conv(w_ref, hist_ref, cw_ref, cb_ref, u_out_ref):
        u = _dot(h, w_ref[...])
        u_out_ref[...] = u
        return cb_ref[...] + cw_ref[0:1, :] * hist_ref[0] + cw_ref[1:2, :] * hist_ref[1] + cw_ref[2:3, :] * u

    ca = conv(wa_ref, ha_ref, cwa_ref, cba_ref, ua_ref)
    cg = conv(wg_ref, hg_ref, cwg_ref, cbg_ref, ug_ref)
    contrib = _dot((_silu(cg) * ca).astype(BF16), wout_ref[...])

    @pl.when(c == 0)
    def _():
        o_ref[...] = y + contrib

    @pl.when(c != 0)
    def _():
        o_ref[...] += contrib


def _ffn_sample(y, g, win16, hist_t, cw, cb, wout16, chunk=256):
    n, d = y.shape
    hidden = wout16.shape[0]
    nc = hidden // chunk
    a_col = lambda c: (0, c)
    g_col = lambda c: (0, c + nc)
    return pl.pallas_call(
        _ffn_sample_kernel,
        grid=(nc,),
        in_specs=[_resident((n, d)), _resident((1, d)),
                  pl.BlockSpec((d, chunk), a_col), pl.BlockSpec((d, chunk), g_col),
                  pl.BlockSpec((CONV_HIST, n, chunk), lambda c: (0, 0, c)),
                  pl.BlockSpec((CONV_HIST, n, chunk), lambda c: (0, 0, c + nc)),
                  pl.BlockSpec((CONV_WIDTH, chunk), a_col), pl.BlockSpec((CONV_WIDTH, chunk), g_col),
                  pl.BlockSpec((1, chunk), a_col), pl.BlockSpec((1, chunk), g_col),
                  pl.BlockSpec((chunk, d), lambda c: (c, 0))],
        out_specs=[pl.BlockSpec((n, d), lambda c: (0, 0)),
                   pl.BlockSpec((n, chunk), a_col), pl.BlockSpec((n, chunk), a_col)],
        out_shape=[jax.ShapeDtypeStruct((n, d), F32), jax.ShapeDtypeStruct((n, hidden), F32),
                   jax.ShapeDtypeStruct((n, hidden), F32)],
        compiler_params=_params("arbitrary"),
        name="ffn_sample",
    )(y, g, win16, win16, hist_t, hist_t, cw, cw, cb, cb, wout16)


def _out_proj_kernel(a_ref, w_ref, x_ref, y_ref):
    y_ref[...] = x_ref[...] + _dot(a_ref[...].astype(BF16), w_ref[...])


def _out_proj(a, w16, x, tile=512):
    n, d = x.shape
    tile = min(tile, n)
    return pl.pallas_call(
        _out_proj_kernel,
        grid=(n // tile,),
        in_specs=[pl.BlockSpec((tile, a.shape[1]), lambda t: (t, 0)), _resident(w16.shape),
                  pl.BlockSpec((tile, d), lambda t: (t, 0))],
        out_specs=pl.BlockSpec((tile, d), lambda t: (t, 0)),
        out_shape=jax.ShapeDtypeStruct((n, d), F32),
        compiler_params=_params("arbitrary"),
        name="out_proj",
    )(a, w16, x)


def _head_norm(r, gain, head_dim):
    lanes = 128
    low = lax.broadcasted_iota(jnp.int32, (1, lanes), 1) < head_dim
    cols = []
    for v in range(r.shape[1] // lanes):
        seg = r[:, v * lanes:(v + 1) * lanes]
        sq = seg * seg
        if head_dim == lanes:
            ms = jnp.mean(sq, axis=-1, keepdims=True)
        else:
            lo = jnp.sum(jnp.where(low, sq, 0.0), axis=-1, keepdims=True)
            hi = jnp.sum(jnp.where(low, 0.0, sq), axis=-1, keepdims=True)
            ms = jnp.where(low, lo, hi) * (1.0 / head_dim)
        cols.append(seg * lax.rsqrt(ms + RMS_EPS))
    return jnp.concatenate(cols, axis=1) * gain


def _proj_kernel(x_ref, g_ref, w_ref, gain_ref, *refs, head_dim, q_f32, block_means):
    o16_ref, kv32_ref = refs[:2]
    q32_ref = refs[2] if q_f32 else None
    km_ref = refs[2 + q_f32] if block_means else None
    h_ref = refs[-1]
    j = pl.program_id(2)
    c = j % 3

    @pl.when(j == 0)
    def _():
        h_ref[...] = _rms(x_ref[0], g_ref[...]).astype(BF16)

    r = _dot(h_ref[...], w_ref[...])

    @pl.when(c < 2)
    def _():
        rn = _head_norm(r, gain_ref[0], head_dim)
        o16_ref[0] = rn.astype(BF16)

        @pl.when(c == 1)
        def _():
            kv32_ref[0] = rn
            if block_means:
                for blk in range(rn.shape[0] // MOBA_BLOCK):
                    rows = rn[blk * MOBA_BLOCK:(blk + 1) * MOBA_BLOCK, :]
                    km_ref[0, blk] = jnp.mean(rows, axis=0, keepdims=True)

        if q_f32:
            @pl.when(c == 0)
            def _():
                q32_ref[0] = rn

    @pl.when(c == 2)
    def _():
        o16_ref[0] = r.astype(BF16)
        kv32_ref[0] = r


def _proj(x, g, w16, gains, tail_rows, head_dim=128, q_f32=False, block_means=False, tile=1024):
    b, s, d = x.shape
    nj = gains.shape[0]
    width = w16.shape[1] // nj
    tile = min(tile, s, tail_rows)
    t0 = (s - tail_rows) // tile

    def kv_map(i, t, j):
        col = (j // 3) * 2 + jnp.maximum(j % 3 - 1, 0)
        return i, jnp.maximum(t - t0, 0), jnp.where(t >= t0, col, 0)

    out_specs = [pl.BlockSpec((1, tile, width), lambda i, t, j: (i, t, j)),
                 pl.BlockSpec((1, tile, width), kv_map)]
    out_shape = [jax.ShapeDtypeStruct((b, s, nj * width), BF16),
                 jax.ShapeDtypeStruct((b, tail_rows, (nj // 3) * 2 * width), F32)]
    if q_f32:
        out_specs.append(pl.BlockSpec((1, tile, width), lambda i, t, j: (i, t, 0)))
        out_shape.append(jax.ShapeDtypeStruct((b, s, width), F32))
    if block_means:
        out_specs.append(pl.BlockSpec((1, tile // MOBA_BLOCK, 1, width), lambda i, t, j: (i, t, 0, 0)))
        out_shape.append(jax.ShapeDtypeStruct((b, s // MOBA_BLOCK, 1, width), F32))
    return pl.pallas_call(
        functools.partial(_proj_kernel, head_dim=head_dim, q_f32=q_f32, block_means=block_means),
        grid=(b, s // tile, nj),
        in_specs=[pl.BlockSpec((1, tile, d), lambda i, t, j: (i, t, 0)), _resident((1, d)),
                  pl.BlockSpec((d, width), lambda i, t, j: (0, j)),
                  pl.BlockSpec((1, 1, width), lambda i, t, j: (j, 0, 0))],
        out_specs=out_specs,
        out_shape=out_shape,
        scratch_shapes=[pltpu.VMEM((tile, d), BF16)],
        compiler_params=_params("arbitrary", "arbitrary", "arbitrary"),
        name="qkv_proj",
    )(x, g, w16, gains)


def _dil_attn_kernel(q_ref, kp_ref, ko_ref, vp_ref, vo_ref, o_ref, lse_ref, *, dil, slopes, head_dim):
    n = pl.program_id(2)
    band = DIL_BAND
    scale = head_dim ** -0.5
    qi = lax.broadcasted_iota(jnp.int32, (band, band), 0)
    kj = lax.broadcasted_iota(jnp.int32, (band, band), 1)
    nd_own = jnp.where(kj <= qi, ((kj - qi) * dil).astype(F32), NEG_INF)
    nd_prev = jnp.where((kj >= qi) & (n > 0), ((kj - qi - band) * dil).astype(F32), NEG_INF)
    for h, slope in enumerate(slopes):
        cols = slice(h * head_dim, (h + 1) * head_dim)
        q = q_ref[0, :, cols]
        s_o = _dot_nt(q, ko_ref[0, :, cols]) * scale + slope * nd_own
        s_p = _dot_nt(q, kp_ref[0, :, cols]) * scale + slope * nd_prev
        m = jnp.maximum(jnp.max(s_o, axis=-1, keepdims=True), jnp.max(s_p, axis=-1, keepdims=True))
        p_o = jnp.exp(s_o - m)
        p_p = jnp.exp(s_p - m)
        l = jnp.sum(p_o, axis=-1, keepdims=True) + jnp.sum(p_p, axis=-1, keepdims=True)
        acc = _dot(p_o.astype(BF16), vo_ref[0, :, cols]) + _dot(p_p.astype(BF16), vp_ref[0, :, cols])
        o_ref[0, :, cols] = acc / l
        lse_ref[0, :, cols] = jnp.broadcast_to(m + jnp.log(l), (band, head_dim))


def _dil_attn(qkv16, g):
    b, s, ncol = qkv16.shape
    dil = DIL_PATTERNS[g][1]
    width = ncol // (3 * len(DIL_PATTERNS))
    head_dim = width // DIL_HEADS
    n_sub = s // dil
    n_blk = n_sub // DIL_BAND
    nj = ncol // width
    view = qkv16.reshape(b, n_sub, dil * ncol)
    slopes = _alibi_slopes(len(DIL_PATTERNS) * DIL_HEADS)[g * DIL_HEADS:(g + 1) * DIL_HEADS]

    def spec(c, prev):
        def index(i, r, n):
            return i, (jnp.maximum(n - 1, 0) if prev else n), r * nj + g * 3 + c
        return pl.BlockSpec((1, DIL_BAND, width), index)

    out_spec = pl.BlockSpec((1, DIL_BAND, width), lambda i, r, n: (i, n, r))
    o, lse = pl.pallas_call(
        functools.partial(_dil_attn_kernel, dil=dil, slopes=slopes, head_dim=head_dim),
        grid=(b, dil, n_blk),
        in_specs=[spec(0, False), spec(1, True), spec(1, False), spec(2, True), spec(2, False)],
        out_specs=[out_spec, out_spec],
        out_shape=[jax.ShapeDtypeStruct((b, n_sub, dil * width), F32)] * 2,
        compiler_params=_params("arbitrary", "arbitrary", "arbitrary"),
        name=f"dil_attn{g}",
    )(view, view, view, view, view)
    return o.reshape(b, s, width), lse.reshape(b, s, width)


def _dil_merge_out_kernel(o0_ref, o1_ref, o2_ref, l0_ref, l1_ref, l2_ref, w_ref, x_ref, y_ref):
    ls = [l0_ref[...], l1_ref[...], l2_ref[...]]
    m = jnp.maximum(jnp.maximum(ls[0], ls[1]), ls[2])
    es = [jnp.exp(l - m) for l in ls]
    den = es[0] + es[1] + es[2]
    merged = (es[0] * o0_ref[...] + es[1] * o1_ref[...] + es[2] * o2_ref[...]) / den
    y_ref[...] = x_ref[...] + _dot(merged.astype(BF16), w_ref[...])


def _dil_merge_out(outs, lses, w16, x, tile=512):
    b, s, d = x.shape
    n = b * s
    tile = min(tile, n)
    row = pl.BlockSpec((tile, d), lambda t: (t, 0))
    flat = [a.reshape(n, d) for a in (*outs, *lses)]
    y = pl.pallas_call(
        _dil_merge_out_kernel,
        grid=(n // tile,),
        in_specs=[row] * 6 + [_resident(w16.shape), row],
        out_specs=row,
        out_shape=jax.ShapeDtypeStruct((n, d), F32),
        compiler_params=_params("arbitrary"),
        name="dil_merge_out",
    )(*flat, w16, x.reshape(n, d))
    return y.reshape(b, s, d)


def _dil_decode_kernel(q_ref, kvn_ref, slope_ref, c0_ref, c1_ref, c2_ref, o_ref, *, head_dim):
    width = DIL_HEADS * head_dim
    scale = head_dim ** -0.5
    sub = lax.broadcasted_iota(jnp.int32, (DIL_HEADS, width), 0)
    lane_head = lax.broadcasted_iota(jnp.int32, (DIL_HEADS, width), 1) // head_dim
    own_head = sub == lane_head
    steps = (DIL_BAND - lax.broadcasted_iota(jnp.int32, (DIL_HEADS, DIL_BAND), 1)).astype(F32)
    outs, lses = [], []
    for g, c_ref in enumerate((c0_ref, c1_ref, c2_ref)):
        dil = DIL_PATTERNS[g][1]
        q = q_ref[0, :, g * 3 * width:g * 3 * width + width].astype(F32)
        q8 = jnp.where(own_head, jnp.broadcast_to(q, (DIL_HEADS, width)), 0.0)
        kn = kvn_ref[0, :, g * 2 * width:g * 2 * width + width].astype(BF16).astype(F32)
        vn = kvn_ref[0, :, g * 2 * width + width:(g + 1) * 2 * width].astype(BF16).astype(F32)
        kc = c_ref[0, :, 0:width].astype(BF16)
        vc = c_ref[0, :, width:2 * width].astype(BF16)
        s = _dot_nt(q8.astype(BF16), kc) * scale - slope_ref[g] * (steps * float(dil))
        s_new = jnp.sum(q8 * kn, axis=-1, keepdims=True) * scale
        m = jnp.maximum(jnp.max(s, axis=-1, keepdims=True), s_new)
        p = jnp.exp(s - m)
        p_new = jnp.exp(s_new - m)
        l = jnp.sum(p, axis=-1, keepdims=True) + p_new
        acc = _dot(p.astype(BF16), vc) + p_new.astype(BF16).astype(F32) * vn
        outs.append(acc / l)
        lses.append(m + jnp.log(l))
    m = jnp.maximum(jnp.maximum(lses[0], lses[1]), lses[2])
    es = [jnp.exp(l - m) for l in lses]
    merged = (es[0] * outs[0] + es[1] * outs[1] + es[2] * outs[2]) / (es[0] + es[1] + es[2])
    o_ref[0] = jnp.sum(jnp.where(own_head, merged, 0.0), axis=0, keepdims=True)


def _dil_decode(q16, kv_new, caches):
    n = q16.shape[0]
    width = kv_new.shape[-1] // (2 * len(DIL_PATTERNS))
    head_dim = width // DIL_HEADS
    views, specs = [], []
    for g, (win, dil) in enumerate(DIL_PATTERNS):
        assert caches[g].shape[1] == win, "the window buffers must be full"
        views.append(caches[g].reshape(n, DIL_BAND, dil * 2 * width))
        specs.append(pl.BlockSpec((1, DIL_BAND, 2 * width), lambda i: (i, 0, 0)))
    slopes = jnp.asarray(_alibi_slopes(len(DIL_PATTERNS) * DIL_HEADS), F32).reshape(len(DIL_PATTERNS), DIL_HEADS, 1)
    slopes = jnp.broadcast_to(slopes, (len(DIL_PATTERNS), DIL_HEADS, DIL_BAND))
    o = pl.pallas_call(
        functools.partial(_dil_decode_kernel, head_dim=head_dim),
        grid=(n,),
        in_specs=[pl.BlockSpec((1, 1, q16.shape[-1]), lambda i: (i, 0, 0)),
                  pl.BlockSpec((1, 1, kv_new.shape[-1]), lambda i: (i, 0, 0)),
                  _resident(slopes.shape)] + specs,
        out_specs=pl.BlockSpec((1, 1, width), lambda i: (i, 0, 0)),
        out_shape=jax.ShapeDtypeStruct((n, 1, width), F32),
        compiler_params=_params("arbitrary"),
        name="dil_decode",
    )(q16, kv_new, slopes, *views)
    return o.reshape(n, width)


def _tiled_gain(gain, heads):
    return jnp.tile(gain.astype(F32), heads)


def _dil_layer(yp, ys, caches, g_mix, w_qkv, q_gain, k_gain, w_o):
    b, s, d = yp.shape
    n = ys.shape[0]
    groups = len(DIL_PATTERNS)
    width = w_qkv.shape[1] // (3 * groups)
    head_dim = width // DIL_HEADS
    w16 = w_qkv.astype(BF16)
    wo16 = w_o.astype(BF16)
    ones = jnp.ones((width,), F32)
    gains = jnp.stack([row for g in range(groups)
                       for row in (_tiled_gain(q_gain[g], DIL_HEADS), _tiled_gain(k_gain[g], DIL_HEADS), ones)])
    gains = gains.reshape(3 * groups, 1, width)
    tail = min(max(w for w, _ in DIL_PATTERNS), s)
    qkv16, kv32 = _proj(yp, g_mix, w16, gains, tail_rows=tail)
    attn = [_dil_attn(qkv16, g) for g in range(groups)]
    yp_new = _dil_merge_out([a[0] for a in attn], [a[1] for a in attn], wo16, yp)
    rows_p = [kv32[:, tail - min(w, s):, g * 2 * width:(g + 1) * 2 * width].reshape(b, min(w, s), 2, DIL_HEADS, head_dim)
              for g, (w, _) in enumerate(DIL_PATTERNS)]
    qs16, kvs32 = _proj(ys[None], g_mix, w16, gains, tail_rows=n)
    att = _dil_decode(qs16.reshape(n, 1, -1), kvs32.reshape(n, 1, -1), caches)
    ys_new = _out_proj(att, wo16, ys)
    kvs = kvs32.reshape(n, 1, groups, 2, DIL_HEADS, head_dim)
    rows_s = [kvs[:, :, g] for g in range(groups)]
    return yp_new, ys_new, rows_p, rows_s


def _top_blocks(gate, block_idx, axis):
    n = gate.shape[axis]
    chosen = jnp.zeros(gate.shape, jnp.bool_)
    for _ in range(MOBA_TOPK):
        mx = jnp.max(gate, axis=axis, keepdims=True)
        idx = jnp.min(jnp.where(gate == mx, block_idx, n), axis=axis, keepdims=True)
        hit = (block_idx == idx) & (mx > NEG_INF)
        chosen = chosen | hit
        gate = jnp.where(block_idx == idx, NEG_INF, gate)
    return chosen


def _moba_attn_kernel(ti_ref, tj_ref, q_ref, k_ref, v_ref, km_ref, o_ref,
                      qm_ref, bits_ref, m_ref, l_ref, acc_ref, *, slopes, head_dim):
    step = pl.program_id(1)
    i = ti_ref[step]
    j = tj_ref[step]
    blk = MOBA_BLOCK
    lanes = 2 * head_dim
    pairs = len(slopes) // 2
    scale = head_dim ** -0.5
    low = lax.broadcasted_iota(jnp.int32, (1, lanes), 1) < head_dim
    rq = lax.broadcasted_iota(jnp.int32, (blk, blk), 0)
    rk = lax.broadcasted_iota(jnp.int32, (blk, blk), 1)

    @pl.when(j == i)
    def _():
        nb = km_ref.shape[1]
        bidx = lax.broadcasted_iota(jnp.int32, (blk, nb), 1)
        for pair in range(pairs):
            cols = slice(pair * lanes, (pair + 1) * lanes)
            qp = q_ref[0, :, cols]
            kmp = km_ref[0, :, cols]
            for half in range(2):
                h = 2 * pair + half
                qh = jnp.where(low if half == 0 else ~low, qp, 0.0)
                qm_ref[h] = qh.astype(BF16)
                gate = _dot_nt(qh, kmp, precision=lax.Precision.HIGHEST)
                chosen = _top_blocks(jnp.where(bidx < i, gate, NEG_INF), bidx, 1)
                bits = jnp.sum(jnp.where(chosen, jnp.left_shift(1, bidx), 0), axis=1, keepdims=True)
                bits_ref[h] = jnp.broadcast_to(bits, (blk, lanes))

    def attend(first):
        key_off = (rk - rq + (j - i) * blk).astype(F32)
        for pair in range(pairs):
            cols = slice(pair * lanes, (pair + 1) * lanes)
            kp = k_ref[0, :, cols]
            vp = v_ref[0, :, cols]
            res, alphas = [], []
            for half in range(2):
                h = 2 * pair + half
                s = _dot_nt(qm_ref[h], kp) * scale + slopes[h] * key_off
                if first:
                    s = jnp.where(rk <= rq, s, NEG_INF)
                else:
                    picked = (jnp.right_shift(bits_ref[h], j) & 1) == 1
                    s = jnp.where(jnp.concatenate([picked] * (blk // lanes), axis=1), s, NEG_INF)
                mx = jnp.max(s, axis=-1, keepdims=True)
                if first:
                    m_new = mx
                    p = jnp.exp(s - m_new)
                    l_new = jnp.sum(p, axis=-1, keepdims=True)
                    alphas.append(None)
                else:
                    m_prev = m_ref[h]
                    m_new = jnp.maximum(m_prev, mx)
                    alpha = jnp.exp(m_prev - m_new)
                    p = jnp.exp(s - m_new)
                    l_new = alpha * l_ref[h] + jnp.sum(p, axis=-1, keepdims=True)
                    alphas.append(alpha)
                m_ref[h] = m_new
                l_ref[h] = l_new
                res.append(_dot(p.astype(BF16), vp))
            new = jnp.where(low, res[0], res[1])
            if first:
                acc_ref[:, cols] = new
            else:
                acc_ref[:, cols] = jnp.where(low, alphas[0], alphas[1]) * acc_ref[:, cols] + new

    @pl.when(j == i)
    def _():
        attend(True)

    @pl.when(j != i)
    def _():
        attend(False)

    @pl.when((j == i - 1) | (i == 0))
    def _():
        for pair in range(pairs):
            cols = slice(pair * lanes, (pair + 1) * lanes)
            o_ref[0, :, cols] = acc_ref[:, cols] / jnp.where(low, l_ref[2 * pair], l_ref[2 * pair + 1])


def _moba_attn(q32, qkv16, km):
    b, s, width = q32.shape
    head_dim = width // MOBA_HEADS
    nt = s // MOBA_BLOCK
    assert nt <= 32, "the chosen (strictly earlier) blocks are kept as bits 0..30 of an int32"
    ti = [i for i in range(nt) for _ in range(i + 1)]
    tj = [j for i in range(nt) for j in [i, *range(i)]]
    tile = (1, MOBA_BLOCK, width)
    grid_spec = pltpu.PrefetchScalarGridSpec(
        num_scalar_prefetch=2,
        grid=(b, len(ti)),
        in_specs=[pl.BlockSpec(tile, lambda n, t, ti, tj: (n, ti[t], 0)),
                  pl.BlockSpec(tile, lambda n, t, ti, tj: (n, tj[t], 1)),
                  pl.BlockSpec(tile, lambda n, t, ti, tj: (n, tj[t], 2)),
                  pl.BlockSpec((1, nt, width), lambda n, t, ti, tj: (n, 0, 0))],
        out_specs=pl.BlockSpec(tile, lambda n, t, ti, tj: (n, ti[t], 0)),
        scratch_shapes=[pltpu.VMEM((MOBA_HEADS, MOBA_BLOCK, 2 * head_dim), BF16),
                        pltpu.VMEM((MOBA_HEADS, MOBA_BLOCK, 2 * head_dim), jnp.int32),
                        pltpu.VMEM((MOBA_HEADS, MOBA_BLOCK, 1), F32),
                        pltpu.VMEM((MOBA_HEADS, MOBA_BLOCK, 1), F32),
                        pltpu.VMEM((MOBA_BLOCK, width), F32)])
    return pl.pallas_call(
        functools.partial(_moba_attn_kernel, slopes=_alibi_slopes(MOBA_HEADS), head_dim=head_dim),
        grid_spec=grid_spec,
        out_shape=jax.ShapeDtypeStruct((b, s, width), F32),
        compiler_params=_params("arbitrary", "arbitrary"),
        name="moba_attn",
    )(jnp.asarray(ti, jnp.int32), jnp.asarray(tj, jnp.int32), q32, qkv16, qkv16, km)


def _moba_decode_kernel(pt_ref, q_ref, kvn_ref, slope_ref, seg_ref, exp_ref, p0_ref, p1_ref, o_ref,
                        qbd_ref, km_ref, m_ref, l_ref, acc_ref, *, head_dim, past_len):
    j = pl.program_id(1)
    width = q_ref.shape[-1]
    heads = width // head_dim
    blk = MOBA_BLOCK
    sub = 8
    scale = head_dim ** -0.5
    hi = lax.Precision.HIGHEST

    @pl.when(j == 0)
    def _():
        own_head = (lax.broadcasted_iota(jnp.int32, (heads, width), 0)
                    == lax.broadcasted_iota(jnp.int32, (heads, width), 1) // head_dim)
        qbd_ref[...] = jnp.where(own_head, jnp.broadcast_to(q_ref[0], (heads, width)), 0.0).astype(BF16)

    k = jnp.concatenate([p0_ref[0, :, 0:width], p1_ref[0, :, 0:width]], axis=0)
    v = jnp.concatenate([p0_ref[0, :, width:2 * width], p1_ref[0, :, width:2 * width]], axis=0)
    dist = (past_len - j * blk - lax.broadcasted_iota(jnp.int32, (blk, 1), 0)).astype(F32)
    s = _dot_nt(k.astype(BF16), qbd_ref[...]) * scale - dist * slope_ref[...]
    m = jnp.max(s, axis=0, keepdims=True)
    p = jnp.exp(s - m)
    l = jnp.sum(p, axis=0, keepdims=True)
    acc = jnp.sum(_dot(p.astype(BF16), exp_ref[...].astype(BF16)) * v, axis=0, keepdims=True)
    kmean = jnp.sum(k, axis=0, keepdims=True) * (1.0 / blk)
    here = lax.broadcasted_iota(jnp.int32, (sub, 1), 0) == j % sub
    t = j // sub
    km_ref[t] = jnp.where(here, kmean, km_ref[t])
    acc_ref[t] = jnp.where(here, acc, acc_ref[t])
    m_ref[t] = jnp.where(here, m, m_ref[t])
    l_ref[t] = jnp.where(here, l, l_ref[t])

    @pl.when(j == pl.num_programs(1) - 1)
    def _():
        nb = km_ref.shape[0] * sub
        q = q_ref[0]
        km_all = km_ref[...].reshape(nb, width)
        acc_all = acc_ref[...].reshape(nb, width)
        m_all = m_ref[...].reshape(nb, heads)
        l_all = l_ref[...].reshape(nb, heads)
        seg = seg_ref[...]
        expand = exp_ref[...]
        gate = jnp.dot(km_all * q, seg, precision=hi, preferred_element_type=F32)
        chosen = _top_blocks(gate, lax.broadcasted_iota(jnp.int32, (nb, heads), 0), 0)
        kn = kvn_ref[0, :, 0:width].astype(BF16).astype(F32)
        vn = kvn_ref[0, :, width:2 * width].astype(BF16).astype(F32)
        s_new = jnp.dot(q.astype(BF16).astype(F32) * kn, seg, precision=hi, preferred_element_type=F32) * scale
        m_tot = jnp.maximum(jnp.max(jnp.where(chosen, m_all, NEG_INF), axis=0, keepdims=True), s_new)
        w = jnp.where(chosen, jnp.exp(m_all - m_tot), 0.0)
        e_new = jnp.exp(s_new - m_tot)
        den = jnp.sum(w * l_all, axis=0, keepdims=True) + e_new
        w_x = jnp.dot(w, expand, precision=hi, preferred_element_type=F32)
        e_x = jnp.dot(e_new.astype(BF16).astype(F32), expand, precision=hi, preferred_element_type=F32)
        den_x = jnp.dot(den, expand, precision=hi, preferred_element_type=F32)
        o_ref[0] = (jnp.sum(w_x * acc_all, axis=0, keepdims=True) + e_x * vn) / den_x


def _moba_decode(q32, kv_new, cache, page_table):
    n, _, width = q32.shape
    head_dim = width // MOBA_HEADS
    n_pages = page_table.shape[1]
    per_blk = MOBA_BLOCK // PAGE_SIZE
    assert cache.shape[1] == PAGE_SIZE and per_blk == 2 and n_pages % (8 * per_blk) == 0
    nb = n_pages // per_blk
    view = cache.reshape(cache.shape[0], PAGE_SIZE, 2 * width)
    slopes = jnp.asarray(_alibi_slopes(MOBA_HEADS), F32).reshape(1, MOBA_HEADS)
    seg = (jnp.arange(width)[:, None] // head_dim == jnp.arange(MOBA_HEADS)[None, :]).astype(F32)
    row = lambda i, j, pt: (i, 0, 0)
    const = lambda i, j, pt: (0, 0)
    grid_spec = pltpu.PrefetchScalarGridSpec(
        num_scalar_prefetch=1,
        grid=(n, nb),
        in_specs=[pl.BlockSpec((1, 1, width), row), pl.BlockSpec((1, 1, 2 * width), row),
                  pl.BlockSpec((1, MOBA_HEADS), const), pl.BlockSpec((width, MOBA_HEADS), const),
                  pl.BlockSpec((MOBA_HEADS, width), const),
                  pl.BlockSpec((1, PAGE_SIZE, 2 * width), lambda i, j, pt: (pt[i, 2 * j], 0, 0)),
                  pl.BlockSpec((1, PAGE_SIZE, 2 * width), lambda i, j, pt: (pt[i, 2 * j + 1], 0, 0))],
        out_specs=pl.BlockSpec((1, 1, width), row),
        scratch_shapes=[pltpu.VMEM((MOBA_HEADS, width), BF16),
                        pltpu.VMEM((nb // 8, 8, width), F32),
                        pltpu.VMEM((nb // 8, 8, MOBA_HEADS), F32),
                        pltpu.VMEM((nb // 8, 8, MOBA_HEADS), F32),
                        pltpu.VMEM((nb // 8, 8, width), F32)])
    o = pl.pallas_call(
        functools.partial(_moba_decode_kernel, head_dim=head_dim, past_len=n_pages * PAGE_SIZE),
        grid_spec=grid_spec,
        out_shape=jax.ShapeDtypeStruct((n, 1, width), F32),
        compiler_params=_params("arbitrary", "arbitrary"),
        name="moba_decode",
    )(page_table, q32, kv_new, slopes, seg, seg.T, view, view)
    return o.reshape(n, width)


def _moba_layer(yp, ys, cache, page_table, g_mix, w_qkv, q_gain, k_gain, w_o):
    b, s, d = yp.shape
    n = ys.shape[0]
    width = w_qkv.shape[1] // 3
    head_dim = width // MOBA_HEADS
    w16 = w_qkv.astype(BF16)
    wo16 = w_o.astype(BF16)
    gains = jnp.stack([_tiled_gain(q_gain, MOBA_HEADS), _tiled_gain(k_gain, MOBA_HEADS), jnp.ones((width,), F32)])
    gains = gains.reshape(3, 1, width)
    qkv16, kv32, q32, km = _proj(yp, g_mix, w16, gains, tail_rows=s, head_dim=head_dim, q_f32=True, block_means=True)
    att = _moba_attn(q32, qkv16, km.reshape(b, s // MOBA_BLOCK, width))
    yp_new = _out_proj(att.reshape(b * s, width), wo16, yp.reshape(b * s, d)).reshape(b, s, d)
    rows_p = kv32.reshape(b, s, 2, MOBA_HEADS, head_dim)
    _, kvs32, qs32 = _proj(ys[None], g_mix, w16, gains, tail_rows=n, head_dim=head_dim, q_f32=True)
    att_s = _moba_decode(qs32.reshape(n, 1, width), kvs32.reshape(n, 1, 2 * width), cache, page_table)
    ys_new = _out_proj(att_s, wo16, ys)
    rows_s = kvs32.reshape(n, 1, 2, MOBA_HEADS, head_dim)
    return yp_new, ys_new, rows_p, rows_s


def kernel(x_prompt, x_sample, state_pool, cache_dil0, cache_dil1, cache_dil2, state_conv, cache_moba, page_table,
           norm_mix, norm_ffn, pool_w, pool_scale, dil_w_qkv, dil_q_gain, dil_k_gain, dil_w_o,
           moba_w_qkv, moba_q_gain, moba_k_gain, moba_w_o, ffn_w_in, ffn_conv_w, ffn_conv_b, ffn_w_out):
    assert x_sample.shape[1] == 1, "the sample group decodes one token per sequence"
    depth = norm_mix.shape[0]
    yp, ys = x_prompt, x_sample[:, 0]
    pool_p, pool_s, moba_p, moba_s, conv_p, conv_s = [], [], [], [], [], []
    dil_p = [[] for _ in DIL_PATTERNS]
    dil_s = [[] for _ in DIL_PATTERNS]
    for layer in range(depth):
        kind, j = layer % N_MIXERS, layer // N_MIXERS
        g_mix = norm_mix[layer][None]
        if kind == 0:
            w16 = pool_w[j].astype(BF16)
            scale = pool_scale[j][None]
            yp, tail = _pool_prompt(yp, g_mix, w16, scale)
            ys, hs = _pool_sample(ys, state_pool[j].transpose(1, 0, 2), g_mix, w16, scale)
            pool_p.append(tail[:, POOL_CARRY - POOL_HIST:])
            pool_s.append(jnp.concatenate([state_pool[j][:, 1:], hs[:, None]], axis=1))
        elif kind == 1:
            caches = [c[j] for c in (cache_dil0, cache_dil1, cache_dil2)]
            yp, ys, rows_p, rows_s = _dil_layer(yp, ys, caches, g_mix, dil_w_qkv[j], dil_q_gain[j], dil_k_gain[j],
                                                dil_w_o[j])
            for g in range(len(DIL_PATTERNS)):
                dil_p[g].append(rows_p[g])
                dil_s[g].append(rows_s[g])
        else:
            yp, ys, rows_p, rows_s = _moba_layer(yp, ys, cache_moba[j], page_table, g_mix, moba_w_qkv[j],
                                                 moba_q_gain[j], moba_k_gain[j], moba_w_o[j])
            moba_p.append(rows_p)
            moba_s.append(rows_s)
        g_ffn = norm_ffn[layer][None]
        win16 = ffn_w_in[layer].astype(BF16)
        wout16 = ffn_w_out[layer].astype(BF16)
        conv_w, conv_b = ffn_conv_w[layer], ffn_conv_b[layer][None]
        yp, tail = _ffn_prompt(yp, g_ffn, win16, conv_w, conv_b, wout16)
        ys, ua, ug = _ffn_sample(ys, g_ffn, win16, state_conv[layer].transpose(1, 0, 2), conv_w, conv_b, wout16)
        conv_p.append(tail[:, CONV_CARRY - CONV_HIST:])
        u_new = jnp.concatenate([ua, ug], axis=-1)[:, None]
        conv_s.append(jnp.concatenate([state_conv[layer][:, 1:], u_new], axis=1))
    stack = lambda xs: jnp.stack(xs, axis=0)
    return (yp, ys[:, None], stack(pool_p), stack(pool_s),
            stack(dil_p[0]), stack(dil_p[1]), stack(dil_p[2]),
            stack(dil_s[0]), stack(dil_s[1]), stack(dil_s[2]),
            stack(moba_p), stack(moba_s), stack(conv_p), stack(conv_s))
```

```python
import functools

import jax
import jax.numpy as jnp
from jax import lax
from jax.experimental import pallas as pl
from jax.experimental.pallas import tpu as pltpu

F32 = jnp.float32
BF16 = jnp.bfloat16

N_MIXERS = 3
POOL_WINDOWS = (2, 4, 8, 16)
POOL_HIST = max(POOL_WINDOWS) - 1
POOL_CARRY = 16
DIL_PATTERNS = ((128, 1), (512, 4), (2048, 16))
DIL_BAND = 128
DIL_HEADS = 8
LSE_LANES = 16
MOBA_BLOCK = 256
MOBA_TOPK = 3
MOBA_HEADS = 16
HEAD_GROUP = 16
PAGE_SIZE = 128
CONV_WIDTH = 3
CONV_HIST = CONV_WIDTH - 1
CONV_CARRY = 8
RMS_EPS = 1e-6
NEG_INF = float("-inf")
MASKED_MAX = 1e30
VMEM_LIMIT = 56 * 1024 * 1024


def _params(*sem):
    return pltpu.CompilerParams(dimension_semantics=sem, vmem_limit_bytes=VMEM_LIMIT)


def _resident(shape):
    zeros = (0,) * len(shape)
    return pl.BlockSpec(shape, lambda *_: zeros, pipeline_mode=pl.Buffered(1))


def _rms(x, g):
    return x * lax.rsqrt(jnp.mean(x * x, axis=-1, keepdims=True) + RMS_EPS) * g


def _dot(a, b):
    return jnp.dot(a, b, preferred_element_type=F32)


def _dot_nt(a, b, precision=None):
    return lax.dot_general(a, b, (((1,), (1,)), ((), ())), preferred_element_type=F32, precision=precision)


def _dot_tn(a, b):
    return lax.dot_general(a, b, (((0,), (0,)), ((), ())), preferred_element_type=F32)


def _alibi_slopes(n):
    return [2.0 ** (-8.0 * i / n) for i in range(1, n + 1)]


def _pool_prompt_kernel(x_ref, g_ref, w_ref, sc_ref, y_ref, tail_ref, ext_ref, *, tile):
    t = pl.program_id(1)
    d_model = x_ref.shape[-1]
    gdim = d_model // len(POOL_WINDOWS)

    @pl.when(t == 0)
    def _():
        ext_ref[0:POOL_CARRY, :] = jnp.zeros((POOL_CARRY, d_model), F32)

    x = x_ref[0]
    ext_ref[POOL_CARRY:POOL_CARRY + tile, :] = _rms(x, g_ref[...])
    row = t * tile + lax.broadcasted_iota(jnp.int32, (tile, 1), 0)
    for gi, win in enumerate(POOL_WINDOWS):
        cols = slice(gi * gdim, (gi + 1) * gdim)
        cur = ext_ref[POOL_CARRY:POOL_CARRY + tile, cols]
        acc = cur
        for j in range(1, win):
            acc = acc + ext_ref[POOL_CARRY - j:POOL_CARRY - j + tile, cols]
        cnt = jnp.minimum(row + 1, win).astype(F32)
        d = acc / cnt - cur
        y = _dot(d.astype(BF16), w_ref[gi])
        y_ref[0, :, cols] = x[:, cols] + y * sc_ref[:, cols]
    last = ext_ref[tile:tile + POOL_CARRY, :]
    ext_ref[0:POOL_CARRY, :] = last

    @pl.when(t == pl.num_programs(1) - 1)
    def _():
        tail_ref[0] = last


def _pool_prompt(x, g, w16, sc, tile=512):
    b, s, d = x.shape
    tile = min(tile, s)
    return pl.pallas_call(
        functools.partial(_pool_prompt_kernel, tile=tile),
        grid=(b, s // tile),
        in_specs=[pl.BlockSpec((1, tile, d), lambda i, t: (i, t, 0)),
                  _resident((1, d)), _resident(w16.shape), _resident((1, d))],
        out_specs=[pl.BlockSpec((1, tile, d), lambda i, t: (i, t, 0)),
                   pl.BlockSpec((1, POOL_CARRY, d), lambda i, t: (i, 0, 0))],
        out_shape=[jax.ShapeDtypeStruct((b, s, d), F32), jax.ShapeDtypeStruct((b, POOL_CARRY, d), F32)],
        scratch_shapes=[pltpu.VMEM((POOL_CARRY + tile, d), F32)],
        compiler_params=_params("arbitrary", "arbitrary"),
        name="pool_prompt",
    )(x, g, w16, sc)


def _pool_sample_kernel(x_ref, st_ref, g_ref, w_ref, sc_ref, y_ref, h_ref):
    d_model = x_ref.shape[-1]
    gdim = d_model // len(POOL_WINDOWS)
    x = x_ref[...]
    h = _rms(x, g_ref[...])
    h_ref[...] = h
    for gi, win in enumerate(POOL_WINDOWS):
        cols = slice(gi * gdim, (gi + 1) * gdim)
        cur = h[:, cols]
        acc = cur
        for j in range(1, win):
            acc = acc + st_ref[POOL_HIST - j, :, cols]
        d = acc / float(win) - cur
        y = _dot(d.astype(BF16), w_ref[gi])
        y_ref[:, cols] = x[:, cols] + y * sc_ref[:, cols]


def _pool_sample(x, state_t, g, w16, sc):
    n, d = x.shape
    return pl.pallas_call(
        _pool_sample_kernel,
        out_shape=[jax.ShapeDtypeStruct((n, d), F32), jax.ShapeDtypeStruct((n, d), F32)],
        compiler_params=pltpu.CompilerParams(vmem_limit_bytes=VMEM_LIMIT),
        name="pool_sample",
    )(x, state_t, g, w16, sc)


def _silu(g):
    return g * (1.0 / (1.0 + jnp.exp(-g)))


def _ffn_prompt_kernel(y_ref, g_ref, win_ref, cw_ref, cb_ref, wout_ref, o_ref, tail_ref,
                       u_ref, carry_ref, acc_ref, *, tile, chunk):
    t = pl.program_id(1)
    hidden = wout_ref.shape[0]

    @pl.when(t == 0)
    def _():
        carry_ref[...] = jnp.zeros(carry_ref.shape, F32)

    y = y_ref[0]
    h = _rms(y, g_ref[...]).astype(BF16)
    n_chunks = hidden // chunk

    def up_project(c):
        us = []
        for part in range(2):
            cols = slice(part * hidden + c * chunk, part * hidden + (c + 1) * chunk)
            u = _dot(h, win_ref[:, cols])
            u_ref[c % 2, part, 0:CONV_CARRY, :] = carry_ref[:, cols]
            u_ref[c % 2, part, CONV_CARRY:CONV_CARRY + tile, :] = u
            carry_ref[:, cols] = u[tile - CONV_CARRY:tile, :]
            us.append(u)
        return us

    def conv_gate(c, us):
        conv = []
        for part in range(2):
            cols = slice(part * hidden + c * chunk, part * hidden + (c + 1) * chunk)
            cv = cb_ref[:, cols] + cw_ref[2:3, cols] * us[part]
            cv = cv + cw_ref[1:2, cols] * u_ref[c % 2, part, CONV_CARRY - 1:CONV_CARRY - 1 + tile, :]
            cv = cv + cw_ref[0:1, cols] * u_ref[c % 2, part, CONV_CARRY - 2:CONV_CARRY - 2 + tile, :]
            conv.append(cv)
        return (_silu(conv[1]) * conv[0]).astype(BF16)

    us_next = up_project(0)
    for c in range(n_chunks):
        us = us_next
        if c + 1 < n_chunks:
            us_next = up_project(c + 1)
        contrib = _dot(conv_gate(c, us), wout_ref[c * chunk:(c + 1) * chunk, :])
        if c == 0:
            acc_ref[...] = contrib
        else:
            acc_ref[...] += contrib
    o_ref[0] = y + acc_ref[...]

    @pl.when(t == pl.num_programs(1) - 1)
    def _():
        tail_ref[0] = carry_ref[...]


def _ffn_prompt(y, g, win16, cw, cb, wout16, tile=512, chunk=256):
    b, s, d = y.shape
    hidden = wout16.shape[0]
    tile = min(tile, s)
    return pl.pallas_call(
        functools.partial(_ffn_prompt_kernel, tile=tile, chunk=chunk),
        grid=(b, s // tile),
        in_specs=[pl.BlockSpec((1, tile, d), lambda i, t: (i, t, 0)),
                  _resident((1, d)), _resident(win16.shape), _resident(cw.shape), _resident(cb.shape),
                  _resident(wout16.shape)],
        out_specs=[pl.BlockSpec((1, tile, d), lambda i, t: (i, t, 0)),
                   pl.BlockSpec((1, CONV_CARRY, 2 * hidden), lambda i, t: (i, 0, 0))],
        out_shape=[jax.ShapeDtypeStruct((b, s, d), F32),
                   jax.ShapeDtypeStruct((b, CONV_CARRY, 2 * hidden), F32)],
        scratch_shapes=[pltpu.VMEM((2, 2, CONV_CARRY + tile, chunk), F32),
                        pltpu.VMEM((CONV_CARRY, 2 * hidden), F32),
                        pltpu.VMEM((tile, d), F32)],
        compiler_params=_params("arbitrary", "arbitrary"),
        name="ffn_prompt",
    )(y, g, win16, cw, cb, wout16)


def _ffn_sample_kernel(y_ref, g_ref, wa_ref, wg_ref, ha_ref, hg_ref, cwa_ref, cwg_ref, cba_ref, cbg_ref,
                       wout_ref, o_ref, ua_ref, ug_ref):
    c = pl.program_id(0)
    y = y_ref[...]
    h = _rms(y, g_ref[...]).astype(BF16)

    def conv(w_ref, hist_ref, cw_ref, cb_ref, u_out_ref):
        u = _dot(h, w_ref[...])
        u_out_ref[...] = u
        return cb_ref[...] + cw_ref[0:1, :] * hist_ref[0] + cw_ref[1:2, :] * hist_ref[1] + cw_ref[2:3, :] * u

    ca = conv(wa_ref, ha_ref, cwa_ref, cba_ref, ua_ref)
    cg = conv(wg_ref, hg_ref, cwg_ref, cbg_ref, ug_ref)
    contrib = _dot((_silu(cg) * ca).astype(BF16), wout_ref[...])

    @pl.when(c == 0)
    def _():
        o_ref[...] = y + contrib

    @pl.when(c != 0)
    def _():
        o_ref[...] += contrib


def _ffn_sample(y, g, win16, hist_t, cw, cb, wout16, chunk=256):
    n, d = y.shape
    hidden = wout16.shape[0]
    nc = hidden // chunk
    a_col = lambda c: (0, c)
    g_col = lambda c: (0, c + nc)
    return pl.pallas_call(
        _ffn_sample_kernel,
        grid=(nc,),
        in_specs=[_resident((n, d)), _resident((1, d)),
                  pl.BlockSpec((d, chunk), a_col), pl.BlockSpec((d, chunk), g_col),
                  pl.BlockSpec((CONV_HIST, n, chunk), lambda c: (0, 0, c)),
                  pl.BlockSpec((CONV_HIST, n, chunk), lambda c: (0, 0, c + nc)),
                  pl.BlockSpec((CONV_WIDTH, chunk), a_col), pl.BlockSpec((CONV_WIDTH, chunk), g_col),
                  pl.BlockSpec((1, chunk), a_col), pl.BlockSpec((1, chunk), g_col),
                  pl.BlockSpec((chunk, d), lambda c: (c, 0))],
        out_specs=[pl.BlockSpec((n, d), lambda c: (0, 0)),
                   pl.BlockSpec((n, chunk), a_col), pl.BlockSpec((n, chunk), a_col)],
        out_shape=[jax.ShapeDtypeStruct((n, d), F32), jax.ShapeDtypeStruct((n, hidden), F32),
                   jax.ShapeDtypeStruct((n, hidden), F32)],
        compiler_params=_params("arbitrary"),
        name="ffn_sample",
    )(y, g, win16, win16, hist_t, hist_t, cw, cw, cb, cb, wout16)


def _out_proj_kernel(a_ref, w_ref, x_ref, y_ref):
    y_ref[...] = x_ref[...] + _dot(a_ref[...].astype(BF16), w_ref[...])


def _out_proj(a, w16, x, tile=512):
    n, d = x.shape
    tile = min(tile, n)
    return pl.pallas_call(
        _out_proj_kernel,
        grid=(n // tile,),
        in_specs=[pl.BlockSpec((tile, a.shape[1]), lambda t: (t, 0)), _resident(w16.shape),
                  pl.BlockSpec((tile, d), lambda t: (t, 0))],
        out_specs=pl.BlockSpec((tile, d), lambda t: (t, 0)),
        out_shape=jax.ShapeDtypeStruct((n, d), F32),
        compiler_params=_params("arbitrary"),
        name="out_proj",
    )(a, w16, x)


def _out_proj_heads_kernel(a_ref, w_ref, x_ref, y_ref):
    a = jnp.concatenate([a_ref[0, h] for h in range(a_ref.shape[1])], axis=1)
    y_ref[0] = x_ref[0] + _dot(a.astype(BF16), w_ref[...])


def _out_proj_heads(a, w16, x, tile=512):
    b, heads, s, head_dim = a.shape
    d = x.shape[-1]
    tile = min(tile, s)
    return pl.pallas_call(
        _out_proj_heads_kernel,
        grid=(b, s // tile),
        in_specs=[pl.BlockSpec((1, heads, tile, head_dim), lambda i, t: (i, 0, t, 0)), _resident(w16.shape),
                  pl.BlockSpec((1, tile, d), lambda i, t: (i, t, 0))],
        out_specs=pl.BlockSpec((1, tile, d), lambda i, t: (i, t, 0)),
        out_shape=jax.ShapeDtypeStruct((b, s, d), F32),
        compiler_params=_params("arbitrary", "arbitrary"),
        name="out_proj_heads",
    )(a, w16, x)


def _head_norm(r, gain, head_dim):
    lanes = 128
    low = lax.broadcasted_iota(jnp.int32, (1, lanes), 1) < head_dim
    cols = []
    for v in range(r.shape[1] // lanes):
        seg = r[:, v * lanes:(v + 1) * lanes]
        sq = seg * seg
        if head_dim == lanes:
            ms = jnp.mean(sq, axis=-1, keepdims=True)
        else:
            lo = jnp.sum(jnp.where(low, sq, 0.0), axis=-1, keepdims=True)
            hi = jnp.sum(jnp.where(low, 0.0, sq), axis=-1, keepdims=True)
            ms = jnp.where(low, lo, hi) * (1.0 / head_dim)
        cols.append(seg * lax.rsqrt(ms + RMS_EPS))
    return jnp.concatenate(cols, axis=1) * gain


def _tiled_gain(gain, heads):
    return jnp.tile(gain.astype(F32), heads)


def _dil_proj_kernel(x_ref, g_ref, w_ref, gain_ref, o0_ref, o1_ref, o2_ref, kv32_ref, h_ref, r_ref,
                     *, dils, head_dim):
    j = pl.program_id(2)
    g = j // 3
    c = j % 3
    chunks, tile, lanes = r_ref.shape

    @pl.when(j == 0)
    def _():
        h_ref[...] = _rms(x_ref[0], g_ref[...]).astype(BF16)

    r = _dot(h_ref[...], w_ref[...])

    def emit(val):
        @pl.when(c > 0)
        def _():
            kv32_ref[0] = val

        for gi, (o_ref, dil) in enumerate(zip((o0_ref, o1_ref, o2_ref), dils)):
            @pl.when(g == gi)
            def _(o_ref=o_ref, dil=dil):
                if dil == 1:
                    o_ref[0, 0] = val.astype(BF16)
                    return
                for ch in range(chunks):
                    r_ref[ch] = val[:, ch * lanes:(ch + 1) * lanes]
                for res in range(dil):
                    for ch in range(chunks):
                        o_ref[0, res, :, ch * lanes:(ch + 1) * lanes] = (
                            r_ref[ch, pl.ds(res, tile // dil, stride=dil), :].astype(BF16))

    @pl.when(c < 2)
    def _():
        emit(_head_norm(r, gain_ref[0], head_dim))

    @pl.when(c == 2)
    def _():
        emit(r)


def _dil_proj(x, g, w16, gains, tail_rows, dils, tile=1024):
    b, s, d = x.shape
    groups = len(dils)
    width = w16.shape[1] // (3 * groups)
    tile = min(tile, s, tail_rows)
    t0 = (s - tail_rows) // tile

    def kv_map(i, t, j):
        col = (j // 3) * 2 + jnp.maximum(j % 3 - 1, 0)
        return i, jnp.maximum(t - t0, 0), jnp.where(t >= t0, col, 0)

    def group_spec(gi, dil):
        return pl.BlockSpec((1, dil, tile // dil, width),
                            lambda i, t, j: (i, 0, t, jnp.clip(j - 3 * gi, 0, 2)))

    return pl.pallas_call(
        functools.partial(_dil_proj_kernel, dils=tuple(dils), head_dim=width // DIL_HEADS),
        grid=(b, s // tile, 3 * groups),
        in_specs=[pl.BlockSpec((1, tile, d), lambda i, t, j: (i, t, 0)), _resident((1, d)),
                  pl.BlockSpec((d, width), lambda i, t, j: (0, j)),
                  pl.BlockSpec((1, 1, width), lambda i, t, j: (j, 0, 0))],
        out_specs=[group_spec(gi, dil) for gi, dil in enumerate(dils)]
        + [pl.BlockSpec((1, tile, width), kv_map)],
        out_shape=[jax.ShapeDtypeStruct((b, dil, s // dil, 3 * width), BF16) for dil in dils]
        + [jax.ShapeDtypeStruct((b, tail_rows, groups * 2 * width), F32)],
        scratch_shapes=[pltpu.VMEM((tile, d), BF16), pltpu.VMEM((width // 128, tile, 128), F32)],
        compiler_params=_params("arbitrary", "arbitrary", "arbitrary"),
        name="dil_proj",
    )(x, g, w16, gains)


def _moba_proj_kernel(x_ref, g_ref, w_ref, gain_ref, *refs, head_dim, prompt):
    if prompt:
        wvt_ref, q32_ref, kv32_ref, k16_ref, vt16_ref, km_ref, h_ref = refs
    else:
        q32_ref, kv32_ref, h_ref = refs
    j = pl.program_id(2)

    @pl.when(j == 0)
    def _():
        h_ref[...] = _rms(x_ref[0], g_ref[...]).astype(BF16)

    r = _dot(h_ref[...], w_ref[...])

    @pl.when(j == 0)
    def _():
        q32_ref[0] = _head_norm(r, gain_ref[0], head_dim)

    @pl.when(j == 1)
    def _():
        rn = _head_norm(r, gain_ref[0], head_dim)
        kv32_ref[0] = rn
        if prompt:
            k16_ref[0] = rn.astype(BF16)
            for blk in range(rn.shape[0] // MOBA_BLOCK):
                rows = rn[blk * MOBA_BLOCK:(blk + 1) * MOBA_BLOCK, :]
                km_ref[0, blk] = jnp.mean(rows, axis=0, keepdims=True)

    @pl.when(j == 2)
    def _():
        kv32_ref[0] = r
        if prompt:
            vt16_ref[0] = _dot_nt(wvt_ref[...], h_ref[...]).astype(BF16)


def _moba_proj(x, g, w16, gains, prompt, tile=1024):
    b, s, d = x.shape
    width = w16.shape[1] // 3
    tile = min(tile, s)
    in_specs = [pl.BlockSpec((1, tile, d), lambda i, t, j: (i, t, 0)), _resident((1, d)),
                pl.BlockSpec((d, width), lambda i, t, j: (0, j)),
                pl.BlockSpec((1, 1, width), lambda i, t, j: (j, 0, 0))]
    args = [x, g, w16, gains]
    out_specs = [pl.BlockSpec((1, tile, width), lambda i, t, j: (i, t, 0)),
                 pl.BlockSpec((1, tile, width), lambda i, t, j: (i, t, jnp.maximum(j - 1, 0)))]
    out_shape = [jax.ShapeDtypeStruct((b, s, width), F32), jax.ShapeDtypeStruct((b, s, 2 * width), F32)]
    if prompt:
        in_specs.append(_resident((width, d)))
        args.append(w16[:, 2 * width:].T)
        out_specs += [pl.BlockSpec((1, tile, width), lambda i, t, j: (i, t, 0)),
                      pl.BlockSpec((1, width, tile), lambda i, t, j: (i, 0, t)),
                      pl.BlockSpec((1, tile // MOBA_BLOCK, 1, width), lambda i, t, j: (i, t, 0, 0))]
        out_shape += [jax.ShapeDtypeStruct((b, s, width), BF16), jax.ShapeDtypeStruct((b, width, s), BF16),
                      jax.ShapeDtypeStruct((b, s // MOBA_BLOCK, 1, width), F32)]
    return pl.pallas_call(
        functools.partial(_moba_proj_kernel, head_dim=width // MOBA_HEADS, prompt=prompt),
        grid=(b, s // tile, 3),
        in_specs=in_specs,
        out_specs=out_specs,
        out_shape=out_shape,
        scratch_shapes=[pltpu.VMEM((tile, d), BF16)],
        compiler_params=_params("arbitrary", "arbitrary", "arbitrary"),
        name="moba_proj",
    )(*args)


def _dil_attn_kernel(q_ref, kp_ref, ko_ref, vp_ref, vo_ref, *refs, dil, slopes, head_dim, chained, last):
    refs = list(refs)
    oin_ref, lin_ref = (refs.pop(0), refs.pop(0)) if chained else (None, None)
    o_ref = refs.pop(0)
    l_ref = None if last else refs.pop(0)
    n = pl.program_id(1)
    r = pl.program_id(2)
    band = DIL_BAND
    width = len(slopes) * head_dim
    scale = head_dim ** -0.5
    rows = slice(None) if dil == 1 else pl.ds(r, band, stride=dil)
    qi = lax.broadcasted_iota(jnp.int32, (band, band), 0)
    kj = lax.broadcasted_iota(jnp.int32, (band, band), 1)
    nd_own = jnp.where(kj <= qi, ((kj - qi) * dil).astype(F32), NEG_INF)
    nd_prev = jnp.where((kj >= qi) & (n > 0), ((kj - qi - band) * dil).astype(F32), NEG_INF)
    lane_head = lax.broadcasted_iota(jnp.int32, (1, band), 1) // LSE_LANES
    if chained:
        l_in = lin_ref[0, rows, :]
    lse_out = jnp.zeros((band, band), F32)
    heads = range(len(slopes))
    head_cols = [slice(h * head_dim, (h + 1) * head_dim) for h in heads]
    scores = []
    for h in heads:
        q = q_ref[0, 0, :, head_cols[h]]
        scores.append((_dot_nt(q, ko_ref[0, 0, :, head_cols[h]]) * scale + slopes[h] * nd_own,
                       _dot_nt(q, kp_ref[0, 0, :, head_cols[h]]) * scale + slopes[h] * nd_prev))
    probs = []
    for h, (s_o, s_p) in zip(heads, scores):
        m = jnp.maximum(jnp.max(s_o, axis=-1, keepdims=True), jnp.max(s_p, axis=-1, keepdims=True))
        if chained:
            lse_prev = l_in[:, h * LSE_LANES:h * LSE_LANES + 1]
            m = jnp.maximum(m, lse_prev)
        p_o = jnp.exp(s_o - m)
        p_p = jnp.exp(s_p - m)
        l = jnp.sum(p_o, axis=-1, keepdims=True) + jnp.sum(p_p, axis=-1, keepdims=True)
        w_prev = None
        if chained:
            w_prev = jnp.exp(lse_prev - m)
            l = l + w_prev
        if not last:
            lse_out = jnp.where(lane_head == h, m + jnp.log(l), lse_out)
        probs.append((p_o.astype(BF16), p_p.astype(BF16), l, w_prev))
    for h, (p_o, p_p, l, w_prev) in zip(heads, probs):
        acc = _dot(p_o, vo_ref[0, 0, :, head_cols[h]]) + _dot(p_p, vp_ref[0, 0, :, head_cols[h]])
        if chained:
            acc = acc + w_prev * oin_ref[0, h, rows, :]
        o_ref[0, h, rows, :] = acc / l
    if not last:
        l_ref[0, rows, :] = lse_out


def _dil_attn(qkv, g, state):
    b, dil, n_sub, ncol = qkv.shape
    width = ncol // 3
    head_dim = width // DIL_HEADS
    s = n_sub * dil
    n_blk = n_sub // DIL_BAND
    groups = len(DIL_PATTERNS)
    last = g == groups - 1
    slopes = _alibi_slopes(groups * DIL_HEADS)[g * DIL_HEADS:(g + 1) * DIL_HEADS]

    def spec(c, prev):
        return pl.BlockSpec((1, 1, DIL_BAND, width),
                            lambda i, n, r: (i, r, (jnp.maximum(n - 1, 0) if prev else n), c))

    rows = DIL_BAND * dil
    o_spec = pl.BlockSpec((1, DIL_HEADS, rows, head_dim), lambda i, n, r: (i, 0, n, 0))
    l_spec = pl.BlockSpec((1, rows, DIL_BAND), lambda i, n, r: (i, n, 0))
    in_specs = [spec(0, False), spec(1, True), spec(1, False), spec(2, True), spec(2, False)]
    args = [qkv] * 5
    if state is not None:
        in_specs += [o_spec, l_spec]
        args += list(state)
    out_specs = [o_spec] if last else [o_spec, l_spec]
    out_shape = [jax.ShapeDtypeStruct((b, DIL_HEADS, s, head_dim), F32)]
    if not last:
        out_shape.append(jax.ShapeDtypeStruct((b, s, DIL_BAND), F32))
    return pl.pallas_call(
        functools.partial(_dil_attn_kernel, dil=dil, slopes=slopes, head_dim=head_dim,
                          chained=state is not None, last=last),
        grid=(b, n_blk, dil),
        in_specs=in_specs,
        out_specs=out_specs,
        out_shape=out_shape,
        compiler_params=_params("arbitrary", "arbitrary", "arbitrary"),
        name=f"dil_attn{g}",
    )(*args)


def _dil_decode_kernel(q_ref, kvn_ref, slope_ref, c0_ref, c1_ref, c2_ref, o_ref, *, head_dim):
    scale = head_dim ** -0.5
    steps = (DIL_BAND - lax.broadcasted_iota(jnp.int32, (DIL_BAND, 1, 1), 0)).astype(F32)
    outs, lses = [], []
    for g, c_ref in enumerate((c0_ref, c1_ref, c2_ref)):
        dil = DIL_PATTERNS[g][1]
        q = q_ref[0, g]
        kn = kvn_ref[0, 2 * g]
        vn = kvn_ref[0, 2 * g + 1]
        kc = c_ref[0, :, 0, 0]
        vc = c_ref[0, :, 0, 1]
        slope = slope_ref[g][:, 0:1]
        s = jnp.sum(kc * q[None], axis=-1, keepdims=True) * scale - slope[None] * (steps * float(dil))
        s_new = jnp.sum(kn * q, axis=-1, keepdims=True) * scale
        m = jnp.maximum(jnp.max(s, axis=0), s_new)
        p = jnp.exp(s - m[None])
        p_new = jnp.exp(s_new - m)
        l = jnp.sum(p, axis=0) + p_new
        acc = jnp.sum(p * vc, axis=0) + p_new * vn
        outs.append(acc / l)
        lses.append(m + jnp.log(l))
    m = jnp.maximum(jnp.maximum(lses[0], lses[1]), lses[2])
    es = [jnp.exp(l - m) for l in lses]
    o_ref[0] = (es[0] * outs[0] + es[1] * outs[1] + es[2] * outs[2]) / (es[0] + es[1] + es[2])


def _dil_decode(q, kv_new, caches):
    n, _, heads, head_dim = q.shape
    views, specs = [], []
    for g, (win, dil) in enumerate(DIL_PATTERNS):
        assert caches[g].shape[1] == win, "the window buffers must be full"
        views.append(caches[g].reshape(n, DIL_BAND, dil, 2, heads, head_dim))
        specs.append(pl.BlockSpec((1, DIL_BAND, 1, 2, heads, head_dim), lambda i: (i, 0, 0, 0, 0, 0)))
    slopes = jnp.asarray(_alibi_slopes(len(DIL_PATTERNS) * heads), F32).reshape(len(DIL_PATTERNS), heads, 1)
    slopes = jnp.broadcast_to(slopes, (len(DIL_PATTERNS), heads, head_dim))
    return pl.pallas_call(
        functools.partial(_dil_decode_kernel, head_dim=head_dim),
        grid=(n,),
        in_specs=[pl.BlockSpec((1,) + q.shape[1:], lambda i: (i, 0, 0, 0)),
                  pl.BlockSpec((1,) + kv_new.shape[1:], lambda i: (i, 0, 0, 0)),
                  _resident(slopes.shape)] + specs,
        out_specs=pl.BlockSpec((1, heads, head_dim), lambda i: (i, 0, 0)),
        out_shape=jax.ShapeDtypeStruct((n, heads, head_dim), F32),
        compiler_params=_params("arbitrary"),
        name="dil_decode",
    )(q, kv_new, slopes, *views)


def _dil_layer(yp, ys, caches, g_mix, w_qkv, q_gain, k_gain, w_o):
    b, s, d = yp.shape
    n = ys.shape[0]
    groups = len(DIL_PATTERNS)
    width = w_qkv.shape[1] // (3 * groups)
    head_dim = width // DIL_HEADS
    w16 = w_qkv.astype(BF16)
    wo16 = w_o.astype(BF16)
    ones = jnp.ones((width,), F32)
    gains = jnp.stack([row for g in range(groups)
                       for row in (_tiled_gain(q_gain[g], DIL_HEADS), _tiled_gain(k_gain[g], DIL_HEADS), ones)])
    gains = gains.reshape(3 * groups, 1, width)
    tail = min(max(w for w, _ in DIL_PATTERNS), s)
    *qkv, kv32 = _dil_proj(yp, g_mix, w16, gains, tail, [dil for _, dil in DIL_PATTERNS])
    state = None
    for g in range(groups):
        state = _dil_attn(qkv[g], g, state)
    yp_new = _out_proj_heads(state[0], wo16, yp)
    rows_p = [kv32[:, tail - min(w, s):, g * 2 * width:(g + 1) * 2 * width].reshape(b, min(w, s), 2, DIL_HEADS, head_dim)
              for g, (w, _) in enumerate(DIL_PATTERNS)]
    *qkv_s, kvs32 = _dil_proj(ys[None], g_mix, w16, gains, n, [1] * groups)
    q_s = jnp.stack([a[0, 0, :, :width] for a in qkv_s], axis=1).astype(F32).reshape(n, groups, DIL_HEADS, head_dim)
    att = _dil_decode(q_s, kvs32.reshape(n, 2 * groups, DIL_HEADS, head_dim), caches)
    ys_new = _out_proj(att.reshape(n, width), wo16, ys)
    kvs = kvs32.reshape(n, 1, groups, 2, DIL_HEADS, head_dim)
    rows_s = [kvs[:, :, g] for g in range(groups)]
    return yp_new, ys_new, rows_p, rows_s


def _top_blocks(gate, block_idx, axis):
    n = gate.shape[axis]
    chosen = jnp.zeros(gate.shape, jnp.bool_)
    for _ in range(MOBA_TOPK):
        mx = jnp.max(gate, axis=axis, keepdims=True)
        idx = jnp.min(jnp.where(gate == mx, block_idx, n), axis=axis, keepdims=True)
        hit = (block_idx == idx) & (mx > NEG_INF)
        chosen = chosen | hit
        gate = jnp.where(block_idx == idx, NEG_INF, gate)
    return chosen


def _moba_attn_kernel(ti_ref, tj_ref, q_ref, k_ref, vt_ref, km_ref, wo_ref, x_ref, y_ref,
                      qm_ref, bits_ref, bias_ref, m_ref, l_ref, acc_ref, *, slopes, head_dim):
    step = pl.program_id(1)
    i = ti_ref[step]
    j = tj_ref[step]
    blk = MOBA_BLOCK
    lanes = 2 * head_dim
    heads = len(slopes)
    scale = head_dim ** -0.5
    low = lax.broadcasted_iota(jnp.int32, (1, lanes), 1) < head_dim
    rk = lax.broadcasted_iota(jnp.int32, (blk, blk), 0)
    rq = lax.broadcasted_iota(jnp.int32, (blk, blk), 1)

    @pl.when((pl.program_id(0) == 0) & (step == 0))
    def _():
        key_off = (rk - rq).astype(F32)
        for h in range(heads):
            bias_ref[h] = slopes[h] * key_off

    @pl.when(j == i)
    def _():
        nb = km_ref.shape[1]
        bidx = lax.broadcasted_iota(jnp.int32, (nb, blk), 0)
        for pair in range(heads // 2):
            cols = slice(pair * lanes, (pair + 1) * lanes)
            qp = q_ref[0, :, cols]
            kmp = km_ref[0, :, cols]
            for half in range(2):
                h = 2 * pair + half
                qh = jnp.where(low if half == 0 else ~low, qp, 0.0)
                qm_ref[h] = (qh * scale).astype(BF16)
                gate = _dot_nt(kmp, qh, precision=lax.Precision.HIGHEST)
                chosen = _top_blocks(jnp.where(bidx < i, gate, NEG_INF), bidx, 0)
                bits_ref[h] = jnp.sum(jnp.where(chosen, jnp.left_shift(1, bidx), 0), axis=0, keepdims=True)

    def softmax_step(h, s, first):
        if first:
            s = jnp.where(rk <= rq, s, NEG_INF)
            m_new = jnp.max(s, axis=0, keepdims=True)
            p = jnp.exp(s - m_new)
            l_new = jnp.sum(p, axis=0, keepdims=True)
            alpha = None
        else:
            offset = slopes[h] * ((j - i) * blk).astype(F32)
            picked = (jnp.right_shift(bits_ref[h], j) & 1) == 1
            m_prev = m_ref[h]
            m_new = jnp.where(picked, jnp.maximum(m_prev, jnp.max(s, axis=0, keepdims=True) + offset), m_prev)
            alpha = jnp.exp(m_prev - m_new)
            p = jnp.exp(s - jnp.where(picked, m_new - offset, MASKED_MAX))
            l_new = alpha * l_ref[h] + jnp.sum(p, axis=0, keepdims=True)
        m_ref[h] = m_new
        l_ref[h] = l_new
        return p.astype(BF16), alpha

    def attend(first):
        for h0 in range(0, heads, HEAD_GROUP):
            group = range(h0, h0 + HEAD_GROUP)
            scores = [_dot_nt(k_ref[0, :, (h // 2) * lanes:(h // 2 + 1) * lanes], qm_ref[h]) + bias_ref[h]
                      for h in group]
            probs = [softmax_step(h, s, first) for h, s in zip(group, scores)]
            for h, (p, alpha) in zip(group, probs):
                rows = slice(h * head_dim, (h + 1) * head_dim)
                pv = _dot(vt_ref[0, rows, :], p)
                if first:
                    acc_ref[rows, :] = pv
                else:
                    acc_ref[rows, :] = alpha * acc_ref[rows, :] + pv

    @pl.when(j == i)
    def _():
        attend(True)

    @pl.when(j != i)
    def _():
        attend(False)

    @pl.when((j == i - 1) | (i == 0))
    def _():
        parts = [acc_ref[h * head_dim:(h + 1) * head_dim, :] / l_ref[h] for h in range(heads)]
        out_t = jnp.concatenate(parts, axis=0).astype(BF16)
        y_ref[0] = x_ref[0] + _dot_tn(out_t, wo_ref[...])


def _moba_attn(q32, k16, vt16, km, wo16, x):
    b, s, width = q32.shape
    d = x.shape[-1]
    head_dim = width // MOBA_HEADS
    nt = s // MOBA_BLOCK
    assert nt <= 32, "the chosen (strictly earlier) blocks are kept as bits 0..30 of an int32"
    ti = [i for i in range(nt) for _ in range(i + 1)]
    tj = [j for i in range(nt) for j in [i, *range(i)]]
    q_tile = lambda n, t, ti, tj: (n, ti[t], 0)
    grid_spec = pltpu.PrefetchScalarGridSpec(
        num_scalar_prefetch=2,
        grid=(b, len(ti)),
        in_specs=[pl.BlockSpec((1, MOBA_BLOCK, width), q_tile),
                  pl.BlockSpec((1, MOBA_BLOCK, width), lambda n, t, ti, tj: (n, tj[t], 0)),
                  pl.BlockSpec((1, width, MOBA_BLOCK), lambda n, t, ti, tj: (n, 0, tj[t])),
                  pl.BlockSpec((1, nt, width), lambda n, t, ti, tj: (n, 0, 0)),
                  pl.BlockSpec(wo16.shape, lambda n, t, ti, tj: (0, 0), pipeline_mode=pl.Buffered(1)),
                  pl.BlockSpec((1, MOBA_BLOCK, d), q_tile)],
        out_specs=pl.BlockSpec((1, MOBA_BLOCK, d), q_tile),
        scratch_shapes=[pltpu.VMEM((MOBA_HEADS, MOBA_BLOCK, 2 * head_dim), BF16),
                        pltpu.VMEM((MOBA_HEADS, 1, MOBA_BLOCK), jnp.int32),
                        pltpu.VMEM((MOBA_HEADS, MOBA_BLOCK, MOBA_BLOCK), F32),
                        pltpu.VMEM((MOBA_HEADS, 1, MOBA_BLOCK), F32),
                        pltpu.VMEM((MOBA_HEADS, 1, MOBA_BLOCK), F32),
                        pltpu.VMEM((width, MOBA_BLOCK), F32)])
    return pl.pallas_call(
        functools.partial(_moba_attn_kernel, slopes=_alibi_slopes(MOBA_HEADS), head_dim=head_dim),
        grid_spec=grid_spec,
        out_shape=jax.ShapeDtypeStruct((b, s, d), F32),
        compiler_params=_params("arbitrary", "arbitrary"),
        name="moba_attn",
    )(jnp.asarray(ti, jnp.int32), jnp.asarray(tj, jnp.int32), q32, k16, vt16, km, wo16, x)


def _moba_decode_kernel(pt_ref, q_ref, kvn_ref, slope_ref, p0_ref, p1_ref, o_ref,
                        km_ref, m_ref, l_ref, acc_ref, *, past_len):
    j = pl.program_id(1)
    heads, head_dim = q_ref.shape[1:]
    scale = head_dim ** -0.5
    lanes = 128
    n = MOBA_BLOCK * heads
    q = q_ref[0]
    k3 = jnp.concatenate([p0_ref[0, :, 0], p1_ref[0, :, 0]], axis=0)
    v3 = jnp.concatenate([p0_ref[0, :, 1], p1_ref[0, :, 1]], axis=0)
    k2 = k3.reshape(n, head_dim).astype(BF16)
    v2 = v3.reshape(n, head_dim).astype(BF16)
    cross = _dot_nt(q.astype(BF16), k2)
    pair = lax.broadcasted_iota(jnp.int32, (heads, n), 1)
    own = (pair & (heads - 1)) == lax.broadcasted_iota(jnp.int32, (heads, n), 0)
    key = lax.broadcasted_iota(jnp.int32, (1, n), 1) // heads
    dist = (past_len - j * MOBA_BLOCK - key).astype(F32)
    s = jnp.sum(jnp.where(own, cross, 0.0), axis=0, keepdims=True) * scale - slope_ref[...] * dist

    def over_keys(x, op):
        parts = [x[:, c * lanes:(c + 1) * lanes] for c in range(n // lanes)]
        while len(parts) > 1:
            parts = [op(a, b) for a, b in zip(parts[::2], parts[1::2])]
        y = parts[0]
        shift = heads
        while shift < lanes:
            y = op(y, pltpu.roll(y, shift, axis=1))
            shift *= 2
        return y

    def to_column(y):
        eye = (lax.broadcasted_iota(jnp.int32, (heads, heads), 0)
               == lax.broadcasted_iota(jnp.int32, (heads, heads), 1))
        return jnp.sum(jnp.where(eye, jnp.broadcast_to(y[:, 0:heads], (heads, heads)), 0.0), axis=1, keepdims=True)

    m = over_keys(s, jnp.maximum)
    p = jnp.exp(s - jnp.concatenate([m] * (n // lanes), axis=1))
    p_own = jnp.where(own, jnp.broadcast_to(p, (heads, n)), 0.0).astype(BF16)
    m_ref[j] = to_column(m)
    l_ref[j] = to_column(over_keys(p, jnp.add))
    acc_ref[j] = _dot(p_own, v2)
    km_ref[j] = jnp.sum(k3, axis=0) * (1.0 / MOBA_BLOCK)

    @pl.when(j == pl.num_programs(1) - 1)
    def _():
        nb = km_ref.shape[0]
        gate = jnp.sum(km_ref[...] * q[None], axis=-1, keepdims=True)
        chosen = _top_blocks(gate, lax.broadcasted_iota(jnp.int32, (nb, heads, 1), 0), 0)
        kn = kvn_ref[0, 0]
        vn = kvn_ref[0, 1]
        s_new = jnp.sum(kn * q, axis=-1, keepdims=True) * scale
        m_all = m_ref[...]
        m_tot = jnp.maximum(jnp.max(jnp.where(chosen, m_all, NEG_INF), axis=0), s_new)
        w = jnp.where(chosen, jnp.exp(m_all - m_tot[None]), 0.0)
        e_new = jnp.exp(s_new - m_tot)
        den = jnp.sum(w * l_ref[...], axis=0) + e_new
        o_ref[0] = (jnp.sum(w * acc_ref[...], axis=0) + e_new * vn) / den


def _moba_decode(q, kv_new, cache, page_table):
    n, heads, head_dim = q.shape
    n_pages = page_table.shape[1]
    per_blk = MOBA_BLOCK // PAGE_SIZE
    assert cache.shape[1] == PAGE_SIZE and per_blk == 2 and n_pages % per_blk == 0
    nb = n_pages // per_blk
    assert heads & (heads - 1) == 0 and 128 % heads == 0
    slopes = jnp.tile(jnp.asarray(_alibi_slopes(heads), F32), MOBA_BLOCK)[None]
    row3 = lambda i, j, pt: (i, 0, 0)
    row4 = lambda i, j, pt: (i, 0, 0, 0)
    page = (1, PAGE_SIZE, 2, heads, head_dim)
    grid_spec = pltpu.PrefetchScalarGridSpec(
        num_scalar_prefetch=1,
        grid=(n, nb),
        in_specs=[pl.BlockSpec((1, heads, head_dim), row3), pl.BlockSpec((1, 2, heads, head_dim), row4),
                  pl.BlockSpec(slopes.shape, lambda i, j, pt: (0, 0)),
                  pl.BlockSpec(page, lambda i, j, pt: (pt[i, 2 * j], 0, 0, 0, 0)),
                  pl.BlockSpec(page, lambda i, j, pt: (pt[i, 2 * j + 1], 0, 0, 0, 0))],
        out_specs=pl.BlockSpec((1, heads, head_dim), row3),
        scratch_shapes=[pltpu.VMEM((nb, heads, head_dim), F32),
                        pltpu.VMEM((nb, heads, 1), F32),
                        pltpu.VMEM((nb, heads, 1), F32),
                        pltpu.VMEM((nb, heads, head_dim), F32)])
    return pl.pallas_call(
        functools.partial(_moba_decode_kernel, past_len=n_pages * PAGE_SIZE),
        grid_spec=grid_spec,
        out_shape=jax.ShapeDtypeStruct((n, heads, head_dim), F32),
        compiler_params=_params("arbitrary", "arbitrary"),
        name="moba_decode",
    )(page_table, q, kv_new, slopes, cache, cache)


def _moba_layer(yp, ys, cache, page_table, g_mix, w_qkv, q_gain, k_gain, w_o):
    b, s, d = yp.shape
    n = ys.shape[0]
    width = w_qkv.shape[1] // 3
    head_dim = width // MOBA_HEADS
    w16 = w_qkv.astype(BF16)
    wo16 = w_o.astype(BF16)
    gains = jnp.stack([_tiled_gain(q_gain, MOBA_HEADS), _tiled_gain(k_gain, MOBA_HEADS), jnp.ones((width,), F32)])
    gains = gains.reshape(3, 1, width)
    q32, kv32, k16, vt16, km = _moba_proj(yp, g_mix, w16, gains, prompt=True)
    yp_new = _moba_attn(q32, k16, vt16, km.reshape(b, s // MOBA_BLOCK, width), wo16, yp)
    rows_p = kv32.reshape(b, s, 2, MOBA_HEADS, head_dim)
    qs32, kvs32 = _moba_proj(ys[None], g_mix, w16, gains, prompt=False)
    att_s = _moba_decode(qs32.reshape(n, MOBA_HEADS, head_dim), kvs32.reshape(n, 2, MOBA_HEADS, head_dim),
                         cache, page_table)
    ys_new = _out_proj(att_s.reshape(n, width), wo16, ys)
    rows_s = kvs32.reshape(n, 1, 2, MOBA_HEADS, head_dim)
    return yp_new, ys_new, rows_p, rows_s


def kernel(x_prompt, x_sample, state_pool, cache_dil0, cache_dil1, cache_dil2, state_conv, cache_moba, page_table,
           norm_mix, norm_ffn, pool_w, pool_scale, dil_w_qkv, dil_q_gain, dil_k_gain, dil_w_o,
           moba_w_qkv, moba_q_gain, moba_k_gain, moba_w_o, ffn_w_in, ffn_conv_w, ffn_conv_b, ffn_w_out):
    assert x_sample.shape[1] == 1, "the sample group decodes one token per sequence"
    depth = norm_mix.shape[0]
    yp, ys = x_prompt, x_sample[:, 0]
    pool_p, pool_s, moba_p, moba_s, conv_p, conv_s = [], [], [], [], [], []
    dil_p = [[] for _ in DIL_PATTERNS]
    dil_s = [[] for _ in DIL_PATTERNS]
    for layer in range(depth):
        kind, j = layer % N_MIXERS, layer // N_MIXERS
        g_mix = norm_mix[layer][None]
        if kind == 0:
            w16 = pool_w[j].astype(BF16)
            scale = pool_scale[j][None]
            yp, tail = _pool_prompt(yp, g_mix, w16, scale)
            ys, hs = _pool_sample(ys, state_pool[j].transpose(1, 0, 2), g_mix, w16, scale)
            pool_p.append(tail[:, POOL_CARRY - POOL_HIST:])
            pool_s.append(jnp.concatenate([state_pool[j][:, 1:], hs[:, None]], axis=1))
        elif kind == 1:
            caches = [c[j] for c in (cache_dil0, cache_dil1, cache_dil2)]
            yp, ys, rows_p, rows_s = _dil_layer(yp, ys, caches, g_mix, dil_w_qkv[j], dil_q_gain[j], dil_k_gain[j],
                                                dil_w_o[j])
            for g in range(len(DIL_PATTERNS)):
                dil_p[g].append(rows_p[g])
                dil_s[g].append(rows_s[g])
        else:
            yp, ys, rows_p, rows_s = _moba_layer(yp, ys, cache_moba[j], page_table, g_mix, moba_w_qkv[j],
                                                 moba_q_gain[j], moba_k_gain[j], moba_w_o[j])
            moba_p.append(rows_p)
            moba_s.append(rows_s)
        g_ffn = norm_ffn[layer][None]
        win16 = ffn_w_in[layer].astype(BF16)
        wout16 = ffn_w_out[layer].astype(BF16)
        conv_w, conv_b = ffn_conv_w[layer], ffn_conv_b[layer][None]
        yp, tail = _ffn_prompt(yp, g_ffn, win16, conv_w, conv_b, wout16)
        ys, ua, ug = _ffn_sample(ys, g_ffn, win16, state_conv[layer].transpose(1, 0, 2), conv_w, conv_b, wout16)
        conv_p.append(tail[:, CONV_CARRY - CONV_HIST:])
        u_new = jnp.concatenate([ua, ug], axis=-1)[:, None]
        conv_s.append(jnp.concatenate([state_conv[layer][:, 1:], u_new], axis=1))
    stack = lambda xs: jnp.stack(xs, axis=0)
    return (yp, ys[:, None], stack(pool_p), stack(pool_s),
            stack(dil_p[0]), stack(dil_p[1]), stack(dil_p[2]),
            stack(dil_s[0]), stack(dil_s[1]), stack(dil_s[2]),
            stack(moba_p), stack(moba_s), stack(conv_p), stack(conv_s))
```

```python
import functools

import jax
import jax.numpy as jnp
from jax import lax
from jax.experimental import pallas as pl
from jax.experimental.pallas import tpu as pltpu

F32 = jnp.float32
BF16 = jnp.bfloat16

N_MIXERS = 3
POOL_WINDOWS = (2, 4, 8, 16)
POOL_HIST = max(POOL_WINDOWS) - 1
POOL_CARRY = 16
DIL_PATTERNS = ((128, 1), (512, 4), (2048, 16))
DIL_BAND = 128
DIL_HEADS = 8
LSE_LANES = 16
MOBA_BLOCK = 256
MOBA_TOPK = 3
MOBA_HEADS = 16
HEAD_GROUP = 16
PAGE_SIZE = 128
CONV_WIDTH = 3
CONV_HIST = CONV_WIDTH - 1
CONV_CARRY = 8
RMS_EPS = 1e-6
NEG_INF = float("-inf")
MASKED_MAX = 1e30
VMEM_LIMIT = 56 * 1024 * 1024


def _params(*sem):
    return pltpu.CompilerParams(dimension_semantics=sem, vmem_limit_bytes=VMEM_LIMIT)


def _resident(shape):
    zeros = (0,) * len(shape)
    return pl.BlockSpec(shape, lambda *_: zeros, pipeline_mode=pl.Buffered(1))


def _rms(x, g):
    return x * lax.rsqrt(jnp.mean(x * x, axis=-1, keepdims=True) + RMS_EPS) * g


def _dot(a, b):
    return jnp.dot(a, b, preferred_element_type=F32)


def _dot_nt(a, b, precision=None):
    return lax.dot_general(a, b, (((1,), (1,)), ((), ())), preferred_element_type=F32, precision=precision)


def _dot_tn(a, b):
    return lax.dot_general(a, b, (((0,), (0,)), ((), ())), preferred_element_type=F32)


def _alibi_slopes(n):
    return [2.0 ** (-8.0 * i / n) for i in range(1, n + 1)]


def _pool_prompt_kernel(x_ref, g_ref, w_ref, sc_ref, y_ref, tail_ref, ext_ref, *, tile):
    t = pl.program_id(1)
    d_model = x_ref.shape[-1]
    gdim = d_model // len(POOL_WINDOWS)

    @pl.when(t == 0)
    def _():
        ext_ref[0:POOL_CARRY, :] = jnp.zeros((POOL_CARRY, d_model), F32)

    x = x_ref[0]
    ext_ref[POOL_CARRY:POOL_CARRY + tile, :] = _rms(x, g_ref[...])
    row = t * tile + lax.broadcasted_iota(jnp.int32, (tile, 1), 0)
    for gi, win in enumerate(POOL_WINDOWS):
        cols = slice(gi * gdim, (gi + 1) * gdim)
        cur = ext_ref[POOL_CARRY:POOL_CARRY + tile, cols]
        acc = cur
        for j in range(1, win):
            acc = acc + ext_ref[POOL_CARRY - j:POOL_CARRY - j + tile, cols]
        cnt = jnp.minimum(row + 1, win).astype(F32)
        d = acc / cnt - cur
        y = _dot(d.astype(BF16), w_ref[gi])
        y_ref[0, :, cols] = x[:, cols] + y * sc_ref[:, cols]
    last = ext_ref[tile:tile + POOL_CARRY, :]
    ext_ref[0:POOL_CARRY, :] = last

    @pl.when(t == pl.num_programs(1) - 1)
    def _():
        tail_ref[0] = last


def _pool_prompt(x, g, w16, sc, tile=512):
    b, s, d = x.shape
    tile = min(tile, s)
    return pl.pallas_call(
        functools.partial(_pool_prompt_kernel, tile=tile),
        grid=(b, s // tile),
        in_specs=[pl.BlockSpec((1, tile, d), lambda i, t: (i, t, 0)),
                  _resident((1, d)), _resident(w16.shape), _resident((1, d))],
        out_specs=[pl.BlockSpec((1, tile, d), lambda i, t: (i, t, 0)),
                   pl.BlockSpec((1, POOL_CARRY, d), lambda i, t: (i, 0, 0))],
        out_shape=[jax.ShapeDtypeStruct((b, s, d), F32), jax.ShapeDtypeStruct((b, POOL_CARRY, d), F32)],
        scratch_shapes=[pltpu.VMEM((POOL_CARRY + tile, d), F32)],
        compiler_params=_params("arbitrary", "arbitrary"),
        name="pool_prompt",
    )(x, g, w16, sc)


def _pool_sample_kernel(x_ref, st_ref, g_ref, w_ref, sc_ref, y_ref, h_ref):
    d_model = x_ref.shape[-1]
    gdim = d_model // len(POOL_WINDOWS)
    x = x_ref[...]
    h = _rms(x, g_ref[...])
    h_ref[...] = h
    for gi, win in enumerate(POOL_WINDOWS):
        cols = slice(gi * gdim, (gi + 1) * gdim)
        cur = h[:, cols]
        acc = cur
        for j in range(1, win):
            acc = acc + st_ref[POOL_HIST - j, :, cols]
        d = acc / float(win) - cur
        y = _dot(d.astype(BF16), w_ref[gi])
        y_ref[:, cols] = x[:, cols] + y * sc_ref[:, cols]


def _pool_sample(x, state_t, g, w16, sc):
    n, d = x.shape
    return pl.pallas_call(
        _pool_sample_kernel,
        out_shape=[jax.ShapeDtypeStruct((n, d), F32), jax.ShapeDtypeStruct((n, d), F32)],
        compiler_params=pltpu.CompilerParams(vmem_limit_bytes=VMEM_LIMIT),
        name="pool_sample",
    )(x, state_t, g, w16, sc)


def _silu(g):
    return g * (1.0 / (1.0 + jnp.exp(-g)))


def _ffn_prompt_kernel(y_ref, g_ref, win_ref, cw_ref, cb_ref, wout_ref, o_ref, tail_ref,
                       u_ref, carry_ref, acc_ref, *, tile, chunk):
    t = pl.program_id(1)
    hidden = wout_ref.shape[0]

    @pl.when(t == 0)
    def _():
        carry_ref[...] = jnp.zeros(carry_ref.shape, F32)

    y = y_ref[0]
    h = _rms(y, g_ref[...]).astype(BF16)
    n_chunks = hidden // chunk

    def up_project(c):
        us = []
        for part in range(2):
            cols = slice(part * hidden + c * chunk, part * hidden + (c + 1) * chunk)
            u = _dot(h, win_ref[:, cols])
            u_ref[c % 2, part, 0:CONV_CARRY, :] = carry_ref[:, cols]
            u_ref[c % 2, part, CONV_CARRY:CONV_CARRY + tile, :] = u
            carry_ref[:, cols] = u[tile - CONV_CARRY:tile, :]
            us.append(u)
        return us

    def conv_gate(c, us):
        conv = []
        for part in range(2):
            cols = slice(part * hidden + c * chunk, part * hidden + (c + 1) * chunk)
            cv = cb_ref[:, cols] + cw_ref[2:3, cols] * us[part]
            cv = cv + cw_ref[1:2, cols] * u_ref[c % 2, part, CONV_CARRY - 1:CONV_CARRY - 1 + tile, :]
            cv = cv + cw_ref[0:1, cols] * u_ref[c % 2, part, CONV_CARRY - 2:CONV_CARRY - 2 + tile, :]
            conv.append(cv)
        return (_silu(conv[1]) * conv[0]).astype(BF16)

    us_next = up_project(0)
    for c in range(n_chunks):
        us = us_next
        if c + 1 < n_chunks:
            us_next = up_project(c + 1)
        contrib = _dot(conv_gate(c, us), wout_ref[c * chunk:(c + 1) * chunk, :])
        if c == 0:
            acc_ref[...] = contrib
        else:
            acc_ref[...] += contrib
    o_ref[0] = y + acc_ref[...]

    @pl.when(t == pl.num_programs(1) - 1)
    def _():
        tail_ref[0] = carry_ref[...]


def _ffn_prompt(y, g, win16, cw, cb, wout16, tile=512, chunk=256):
    b, s, d = y.shape
    hidden = wout16.shape[0]
    tile = min(tile, s)
    return pl.pallas_call(
        functools.partial(_ffn_prompt_kernel, tile=tile, chunk=chunk),
        grid=(b, s // tile),
        in_specs=[pl.BlockSpec((1, tile, d), lambda i, t: (i, t, 0)),
                  _resident((1, d)), _resident(win16.shape), _resident(cw.shape), _resident(cb.shape),
                  _resident(wout16.shape)],
        out_specs=[pl.BlockSpec((1, tile, d), lambda i, t: (i, t, 0)),
                   pl.BlockSpec((1, CONV_CARRY, 2 * hidden), lambda i, t: (i, 0, 0))],
        out_shape=[jax.ShapeDtypeStruct((b, s, d), F32),
                   jax.ShapeDtypeStruct((b, CONV_CARRY, 2 * hidden), F32)],
        scratch_shapes=[pltpu.VMEM((2, 2, CONV_CARRY + tile, chunk), F32),
                        pltpu.VMEM((CONV_CARRY, 2 * hidden), F32),
                        pltpu.VMEM((tile, d), F32)],
        compiler_params=_params("arbitrary", "arbitrary"),
        name="ffn_prompt",
    )(y, g, win16, cw, cb, wout16)


def _ffn_sample_kernel(y_ref, g_ref, wa_ref, wg_ref, ha_ref, hg_ref, cwa_ref, cwg_ref, cba_ref, cbg_ref,
                       wout_ref, o_ref, ua_ref, ug_ref):
    c = pl.program_id(0)
    y = y_ref[...]
    h = _rms(y, g_ref[...]).astype(BF16)

    def conv(w_ref, hist_ref, cw_ref, cb_ref, u_out_ref):
        u = _dot(h, w_ref[...])
        u_out_ref[...] = u
        return cb_ref[...] + cw_ref[0:1, :] * hist_ref[0] + cw_ref[1:2, :] * hist_ref[1] + cw_ref[2:3, :] * u

    ca = conv(wa_ref, ha_ref, cwa_ref, cba_ref, ua_ref)
    cg = conv(wg_ref, hg_ref, cwg_ref, cbg_ref, ug_ref)
    contrib = _dot((_silu(cg) * ca).astype(BF16), wout_ref[...])

    @pl.when(c == 0)
    def _():
        o_ref[...] = y + contrib

    @pl.when(c != 0)
    def _():
        o_ref[...] += contrib


def _ffn_sample(y, g, win16, hist_t, cw, cb, wout16, chunk=256):
    n, d = y.shape
    hidden = wout16.shape[0]
    nc = hidden // chunk
    a_col = lambda c: (0, c)
    g_col = lambda c: (0, c + nc)
    return pl.pallas_call(
        _ffn_sample_kernel,
        grid=(nc,),
        in_specs=[_resident((n, d)), _resident((1, d)),
                  pl.BlockSpec((d, chunk), a_col), pl.BlockSpec((d, chunk), g_col),
                  pl.BlockSpec((CONV_HIST, n, chunk), lambda c: (0, 0, c)),
                  pl.BlockSpec((CONV_HIST, n, chunk), lambda c: (0, 0, c + nc)),
                  pl.BlockSpec((CONV_WIDTH, chunk), a_col), pl.BlockSpec((CONV_WIDTH, chunk), g_col),
                  pl.BlockSpec((1, chunk), a_col), pl.BlockSpec((1, chunk), g_col),
                  pl.BlockSpec((chunk, d), lambda c: (c, 0))],
        out_specs=[pl.BlockSpec((n, d), lambda c: (0, 0)),
                   pl.BlockSpec((n, chunk), a_col), pl.BlockSpec((n, chunk), a_col)],
        out_shape=[jax.ShapeDtypeStruct((n, d), F32), jax.ShapeDtypeStruct((n, hidden), F32),
                   jax.ShapeDtypeStruct((n, hidden), F32)],
        compiler_params=_params("arbitrary"),
        name="ffn_sample",
    )(y, g, win16, win16, hist_t, hist_t, cw, cw, cb, cb, wout16)


def _out_proj_kernel(a_ref, w_ref, x_ref, y_ref):
    y_ref[...] = x_ref[...] + _dot(a_ref[...].astype(BF16), w_ref[...])


def _out_proj(a, w16, x, tile=512):
    n, d = x.shape
    tile = min(tile, n)
    return pl.pallas_call(
        _out_proj_kernel,
        grid=(n // tile,),
        in_specs=[pl.BlockSpec((tile, a.shape[1]), lambda t: (t, 0)), _resident(w16.shape),
                  pl.BlockSpec((tile, d), lambda t: (t, 0))],
        out_specs=pl.BlockSpec((tile, d), lambda t: (t, 0)),
        out_shape=jax.ShapeDtypeStruct((n, d), F32),
        compiler_params=_params("arbitrary"),
        name="out_proj",
    )(a, w16, x)


def _out_proj_heads_kernel(a_ref, w_ref, x_ref, y_ref):
    a = jnp.concatenate([a_ref[0, h] for h in range(a_ref.shape[1])], axis=1)
    y_ref[0] = x_ref[0] + _dot(a.astype(BF16), w_ref[...])


def _out_proj_heads(a, w16, x, tile=512):
    b, heads, s, head_dim = a.shape
    d = x.shape[-1]
    tile = min(tile, s)
    return pl.pallas_call(
        _out_proj_heads_kernel,
        grid=(b, s // tile),
        in_specs=[pl.BlockSpec((1, heads, tile, head_dim), lambda i, t: (i, 0, t, 0)), _resident(w16.shape),
                  pl.BlockSpec((1, tile, d), lambda i, t: (i, t, 0))],
        out_specs=pl.BlockSpec((1, tile, d), lambda i, t: (i, t, 0)),
        out_shape=jax.ShapeDtypeStruct((b, s, d), F32),
        compiler_params=_params("arbitrary", "arbitrary"),
        name="out_proj_heads",
    )(a, w16, x)


def _head_norm(r, gain, head_dim):
    lanes = 128
    low = lax.broadcasted_iota(jnp.int32, (1, lanes), 1) < head_dim
    cols = []
    for v in range(r.shape[1] // lanes):
        seg = r[:, v * lanes:(v + 1) * lanes]
        sq = seg * seg
        if head_dim == lanes:
            ms = jnp.mean(sq, axis=-1, keepdims=True)
        else:
            lo = jnp.sum(jnp.where(low, sq, 0.0), axis=-1, keepdims=True)
            hi = jnp.sum(jnp.where(low, 0.0, sq), axis=-1, keepdims=True)
            ms = jnp.where(low, lo, hi) * (1.0 / head_dim)
        cols.append(seg * lax.rsqrt(ms + RMS_EPS))
    return jnp.concatenate(cols, axis=1) * gain


def _tiled_gain(gain, heads):
    return jnp.tile(gain.astype(F32), heads)


def _dil_proj_kernel(x_ref, g_ref, w_ref, gain_ref, o0_ref, o1_ref, o2_ref, kv32_ref, h_ref, r_ref,
                     *, dils, head_dim):
    j = pl.program_id(2)
    g = j // 3
    c = j % 3
    chunks, tile, lanes = r_ref.shape

    @pl.when(j == 0)
    def _():
        h_ref[...] = _rms(x_ref[0], g_ref[...]).astype(BF16)

    r = _dot(h_ref[...], w_ref[...])

    def emit(val):
        @pl.when(c > 0)
        def _():
            kv32_ref[0] = val

        for gi, (o_ref, dil) in enumerate(zip((o0_ref, o1_ref, o2_ref), dils)):
            @pl.when(g == gi)
            def _(o_ref=o_ref, dil=dil):
                if dil == 1:
                    o_ref[0, 0] = val.astype(BF16)
                    return
                for ch in range(chunks):
                    r_ref[ch] = val[:, ch * lanes:(ch + 1) * lanes]
                for res in range(dil):
                    for ch in range(chunks):
                        o_ref[0, res, :, ch * lanes:(ch + 1) * lanes] = (
                            r_ref[ch, pl.ds(res, tile // dil, stride=dil), :].astype(BF16))

    @pl.when(c < 2)
    def _():
        emit(_head_norm(r, gain_ref[0], head_dim))

    @pl.when(c == 2)
    def _():
        emit(r)


def _dil_proj(x, g, w16, gains, tail_rows, dils, tile=1024):
    b, s, d = x.shape
    groups = len(dils)
    width = w16.shape[1] // (3 * groups)
    tile = min(tile, s, tail_rows)
    t0 = (s - tail_rows) // tile

    def kv_map(i, t, j):
        col = (j // 3) * 2 + jnp.maximum(j % 3 - 1, 0)
        return i, jnp.maximum(t - t0, 0), jnp.where(t >= t0, col, 0)

    def group_spec(gi, dil):
        return pl.BlockSpec((1, dil, tile // dil, width),
                            lambda i, t, j: (i, 0, t, jnp.clip(j - 3 * gi, 0, 2)))

    return pl.pallas_call(
        functools.partial(_dil_proj_kernel, dils=tuple(dils), head_dim=width // DIL_HEADS),
        grid=(b, s // tile, 3 * groups),
        in_specs=[pl.BlockSpec((1, tile, d), lambda i, t, j: (i, t, 0)), _resident((1, d)),
                  pl.BlockSpec((d, width), lambda i, t, j: (0, j)),
                  pl.BlockSpec((1, 1, width), lambda i, t, j: (j, 0, 0))],
        out_specs=[group_spec(gi, dil) for gi, dil in enumerate(dils)]
        + [pl.BlockSpec((1, tile, width), kv_map)],
        out_shape=[jax.ShapeDtypeStruct((b, dil, s // dil, 3 * width), BF16) for dil in dils]
        + [jax.ShapeDtypeStruct((b, tail_rows, groups * 2 * width), F32)],
        scratch_shapes=[pltpu.VMEM((tile, d), BF16), pltpu.VMEM((width // 128, tile, 128), F32)],
        compiler_params=_params("arbitrary", "arbitrary", "arbitrary"),
        name="dil_proj",
    )(x, g, w16, gains)


def _moba_proj_kernel(x_ref, g_ref, w_ref, gain_ref, *refs, head_dim, prompt):
    if prompt:
        wvt_ref, q32_ref, kv32_ref, k16_ref, vt16_ref, km_ref, h_ref = refs
    else:
        q32_ref, kv32_ref, h_ref = refs
    j = pl.program_id(2)

    @pl.when(j == 0)
    def _():
        h_ref[...] = _rms(x_ref[0], g_ref[...]).astype(BF16)

    r = _dot(h_ref[...], w_ref[...])

    @pl.when(j == 0)
    def _():
        q32_ref[0] = _head_norm(r, gain_ref[0], head_dim)

    @pl.when(j == 1)
    def _():
        rn = _head_norm(r, gain_ref[0], head_dim)
        kv32_ref[0] = rn
        if prompt:
            k16_ref[0] = rn.astype(BF16)
            for blk in range(rn.shape[0] // MOBA_BLOCK):
                rows = rn[blk * MOBA_BLOCK:(blk + 1) * MOBA_BLOCK, :]
                km_ref[0, blk] = jnp.mean(rows, axis=0, keepdims=True)

    @pl.when(j == 2)
    def _():
        kv32_ref[0] = r
        if prompt:
            vt16_ref[0] = _dot_nt(wvt_ref[...], h_ref[...]).astype(BF16)


def _moba_proj(x, g, w16, gains, prompt, tile=1024):
    b, s, d = x.shape
    width = w16.shape[1] // 3
    tile = min(tile, s)
    in_specs = [pl.BlockSpec((1, tile, d), lambda i, t, j: (i, t, 0)), _resident((1, d)),
                pl.BlockSpec((d, width), lambda i, t, j: (0, j)),
                pl.BlockSpec((1, 1, width), lambda i, t, j: (j, 0, 0))]
    args = [x, g, w16, gains]
    out_specs = [pl.BlockSpec((1, tile, width), lambda i, t, j: (i, t, 0)),
                 pl.BlockSpec((1, tile, width), lambda i, t, j: (i, t, jnp.maximum(j - 1, 0)))]
    out_shape = [jax.ShapeDtypeStruct((b, s, width), F32), jax.ShapeDtypeStruct((b, s, 2 * width), F32)]
    if prompt:
        in_specs.append(_resident((width, d)))
        args.append(w16[:, 2 * width:].T)
        out_specs += [pl.BlockSpec((1, tile, width), lambda i, t, j: (i, t, 0)),
                      pl.BlockSpec((1, width, tile), lambda i, t, j: (i, 0, t)),
                      pl.BlockSpec((1, tile // MOBA_BLOCK, 1, width), lambda i, t, j: (i, t, 0, 0))]
        out_shape += [jax.ShapeDtypeStruct((b, s, width), BF16), jax.ShapeDtypeStruct((b, width, s), BF16),
                      jax.ShapeDtypeStruct((b, s // MOBA_BLOCK, 1, width), F32)]
    return pl.pallas_call(
        functools.partial(_moba_proj_kernel, head_dim=width // MOBA_HEADS, prompt=prompt),
        grid=(b, s // tile, 3),
        in_specs=in_specs,
        out_specs=out_specs,
        out_shape=out_shape,
        scratch_shapes=[pltpu.VMEM((tile, d), BF16)],
        compiler_params=_params("arbitrary", "arbitrary", "arbitrary"),
        name="moba_proj",
    )(*args)


def _dil_attn_kernel(q_ref, kp_ref, ko_ref, vp_ref, vo_ref, *refs, dil, slopes, head_dim, chained, last):
    refs = list(refs)
    oin_ref, lin_ref = (refs.pop(0), refs.pop(0)) if chained else (None, None)
    o_ref = refs.pop(0)
    l_ref = None if last else refs.pop(0)
    n = pl.program_id(1)
    r = pl.program_id(2)
    band = DIL_BAND
    width = len(slopes) * head_dim
    scale = head_dim ** -0.5
    rows = slice(None) if dil == 1 else pl.ds(r, band, stride=dil)
    qi = lax.broadcasted_iota(jnp.int32, (band, band), 0)
    kj = lax.broadcasted_iota(jnp.int32, (band, band), 1)
    nd_own = jnp.where(kj <= qi, ((kj - qi) * dil).astype(F32), NEG_INF)
    nd_prev = jnp.where((kj >= qi) & (n > 0), ((kj - qi - band) * dil).astype(F32), NEG_INF)
    lane_head = lax.broadcasted_iota(jnp.int32, (1, band), 1) // LSE_LANES
    if chained:
        l_in = lin_ref[0, rows, :]
    lse_out = jnp.zeros((band, band), F32)
    heads = range(len(slopes))
    head_cols = [slice(h * head_dim, (h + 1) * head_dim) for h in heads]
    scores = []
    for h in heads:
        q = q_ref[0, 0, :, head_cols[h]]
        scores.append((_dot_nt(q, ko_ref[0, 0, :, head_cols[h]]) * scale + slopes[h] * nd_own,
                       _dot_nt(q, kp_ref[0, 0, :, head_cols[h]]) * scale + slopes[h] * nd_prev))
    probs = []
    for h, (s_o, s_p) in zip(heads, scores):
        m = jnp.maximum(jnp.max(s_o, axis=-1, keepdims=True), jnp.max(s_p, axis=-1, keepdims=True))
        if chained:
            lse_prev = l_in[:, h * LSE_LANES:h * LSE_LANES + 1]
            m = jnp.maximum(m, lse_prev)
        p_o = jnp.exp(s_o - m)
        p_p = jnp.exp(s_p - m)
        l = jnp.sum(p_o, axis=-1, keepdims=True) + jnp.sum(p_p, axis=-1, keepdims=True)
        w_prev = None
        if chained:
            w_prev = jnp.exp(lse_prev - m)
            l = l + w_prev
        if not last:
            lse_out = jnp.where(lane_head == h, m + jnp.log(l), lse_out)
        probs.append((p_o.astype(BF16), p_p.astype(BF16), l, w_prev))
    for h, (p_o, p_p, l, w_prev) in zip(heads, probs):
        acc = _dot(p_o, vo_ref[0, 0, :, head_cols[h]]) + _dot(p_p, vp_ref[0, 0, :, head_cols[h]])
        if chained:
            acc = acc + w_prev * oin_ref[0, h, rows, :]
        o_ref[0, h, rows, :] = acc / l
    if not last:
        l_ref[0, rows, :] = lse_out


def _dil_attn(qkv, g, state):
    b, dil, n_sub, ncol = qkv.shape
    width = ncol // 3
    head_dim = width // DIL_HEADS
    s = n_sub * dil
    n_blk = n_sub // DIL_BAND
    groups = len(DIL_PATTERNS)
    last = g == groups - 1
    slopes = _alibi_slopes(groups * DIL_HEADS)[g * DIL_HEADS:(g + 1) * DIL_HEADS]

    def spec(c, prev):
        return pl.BlockSpec((1, 1, DIL_BAND, width),
                            lambda i, n, r: (i, r, (jnp.maximum(n - 1, 0) if prev else n), c))

    rows = DIL_BAND * dil
    o_spec = pl.BlockSpec((1, DIL_HEADS, rows, head_dim), lambda i, n, r: (i, 0, n, 0))
    l_spec = pl.BlockSpec((1, rows, DIL_BAND), lambda i, n, r: (i, n, 0))
    in_specs = [spec(0, False), spec(1, True), spec(1, False), spec(2, True), spec(2, False)]
    args = [qkv] * 5
    if state is not None:
        in_specs += [o_spec, l_spec]
        args += list(state)
    out_specs = [o_spec] if last else [o_spec, l_spec]
    out_shape = [jax.ShapeDtypeStruct((b, DIL_HEADS, s, head_dim), F32)]
    if not last:
        out_shape.append(jax.ShapeDtypeStruct((b, s, DIL_BAND), F32))
    return pl.pallas_call(
        functools.partial(_dil_attn_kernel, dil=dil, slopes=slopes, head_dim=head_dim,
                          chained=state is not None, last=last),
        grid=(b, n_blk, dil),
        in_specs=in_specs,
        out_specs=out_specs,
        out_shape=out_shape,
        compiler_params=_params("arbitrary", "arbitrary", "arbitrary"),
        name=f"dil_attn{g}",
    )(*args)


def _dil_decode_kernel(q_ref, kvn_ref, slope_ref, c0_ref, c1_ref, c2_ref, o_ref, *, head_dim):
    scale = head_dim ** -0.5
    steps = (DIL_BAND - lax.broadcasted_iota(jnp.int32, (DIL_BAND, 1, 1), 0)).astype(F32)
    outs, lses = [], []
    for g, c_ref in enumerate((c0_ref, c1_ref, c2_ref)):
        dil = DIL_PATTERNS[g][1]
        q = q_ref[0, g]
        kn = kvn_ref[0, 2 * g]
        vn = kvn_ref[0, 2 * g + 1]
        kc = c_ref[0, :, 0, 0]
        vc = c_ref[0, :, 0, 1]
        slope = slope_ref[g][:, 0:1]
        s = jnp.sum(kc * q[None], axis=-1, keepdims=True) * scale - slope[None] * (steps * float(dil))
        s_new = jnp.sum(kn * q, axis=-1, keepdims=True) * scale
        m = jnp.maximum(jnp.max(s, axis=0), s_new)
        p = jnp.exp(s - m[None])
        p_new = jnp.exp(s_new - m)
        l = jnp.sum(p, axis=0) + p_new
        acc = jnp.sum(p * vc, axis=0) + p_new * vn
        outs.append(acc / l)
        lses.append(m + jnp.log(l))
    m = jnp.maximum(jnp.maximum(lses[0], lses[1]), lses[2])
    es = [jnp.exp(l - m) for l in lses]
    o_ref[0] = (es[0] * outs[0] + es[1] * outs[1] + es[2] * outs[2]) / (es[0] + es[1] + es[2])


def _dil_decode(q, kv_new, caches):
    n, _, heads, head_dim = q.shape
    views, specs = [], []
    for g, (win, dil) in enumerate(DIL_PATTERNS):
        assert caches[g].shape[1] == win, "the window buffers must be full"
        views.append(caches[g].reshape(n, DIL_BAND, dil, 2, heads, head_dim))
        specs.append(pl.BlockSpec((1, DIL_BAND, 1, 2, heads, head_dim), lambda i: (i, 0, 0, 0, 0, 0)))
    slopes = jnp.asarray(_alibi_slopes(len(DIL_PATTERNS) * heads), F32).reshape(len(DIL_PATTERNS), heads, 1)
    slopes = jnp.broadcast_to(slopes, (len(DIL_PATTERNS), heads, head_dim))
    return pl.pallas_call(
        functools.partial(_dil_decode_kernel, head_dim=head_dim),
        grid=(n,),
        in_specs=[pl.BlockSpec((1,) + q.shape[1:], lambda i: (i, 0, 0, 0)),
                  pl.BlockSpec((1,) + kv_new.shape[1:], lambda i: (i, 0, 0, 0)),
                  _resident(slopes.shape)] + specs,
        out_specs=pl.BlockSpec((1, heads, head_dim), lambda i: (i, 0, 0)),
        out_shape=jax.ShapeDtypeStruct((n, heads, head_dim), F32),
        compiler_params=_params("arbitrary"),
        name="dil_decode",
    )(q, kv_new, slopes, *views)


def _dil_layer(yp, ys, caches, g_mix, w_qkv, q_gain, k_gain, w_o):
    b, s, d = yp.shape
    n = ys.shape[0]
    groups = len(DIL_PATTERNS)
    width = w_qkv.shape[1] // (3 * groups)
    head_dim = width // DIL_HEADS
    w16 = w_qkv.astype(BF16)
    wo16 = w_o.astype(BF16)
    ones = jnp.ones((width,), F32)
    gains = jnp.stack([row for g in range(groups)
                       for row in (_tiled_gain(q_gain[g], DIL_HEADS), _tiled_gain(k_gain[g], DIL_HEADS), ones)])
    gains = gains.reshape(3 * groups, 1, width)
    tail = min(max(w for w, _ in DIL_PATTERNS), s)
    *qkv, kv32 = _dil_proj(yp, g_mix, w16, gains, tail, [dil for _, dil in DIL_PATTERNS])
    state = None
    for g in range(groups):
        state = _dil_attn(qkv[g], g, state)
    yp_new = _out_proj_heads(state[0], wo16, yp)
    rows_p = [kv32[:, tail - min(w, s):, g * 2 * width:(g + 1) * 2 * width].reshape(b, min(w, s), 2, DIL_HEADS, head_dim)
              for g, (w, _) in enumerate(DIL_PATTERNS)]
    *qkv_s, kvs32 = _dil_proj(ys[None], g_mix, w16, gains, n, [1] * groups)
    q_s = jnp.stack([a[0, 0, :, :width] for a in qkv_s], axis=1).astype(F32).reshape(n, groups, DIL_HEADS, head_dim)
    att = _dil_decode(q_s, kvs32.reshape(n, 2 * groups, DIL_HEADS, head_dim), caches)
    ys_new = _out_proj(att.reshape(n, width), wo16, ys)
    kvs = kvs32.reshape(n, 1, groups, 2, DIL_HEADS, head_dim)
    rows_s = [kvs[:, :, g] for g in range(groups)]
    return yp_new, ys_new, rows_p, rows_s


def _top_blocks(gate, block_idx, axis):
    n = gate.shape[axis]
    chosen = jnp.zeros(gate.shape, jnp.bool_)
    for _ in range(MOBA_TOPK):
        mx = jnp.max(gate, axis=axis, keepdims=True)
        idx = jnp.min(jnp.where(gate == mx, block_idx, n), axis=axis, keepdims=True)
        hit = (block_idx == idx) & (mx > NEG_INF)
        chosen = chosen | hit
        gate = jnp.where(block_idx == idx, NEG_INF, gate)
    return chosen


def _moba_attn_kernel(ti_ref, tj_ref, q_ref, k_ref, vt_ref, km_ref, wo_ref, x_ref, y_ref,
                      qm_ref, bits_ref, bias_ref, m_ref, l_ref, acc_ref, *, slopes, head_dim):
    step = pl.program_id(1)
    i = ti_ref[step]
    j = tj_ref[step]
    blk = MOBA_BLOCK
    lanes = 2 * head_dim
    heads = len(slopes)
    scale = head_dim ** -0.5
    low = lax.broadcasted_iota(jnp.int32, (1, lanes), 1) < head_dim
    rk = lax.broadcasted_iota(jnp.int32, (blk, blk), 0)
    rq = lax.broadcasted_iota(jnp.int32, (blk, blk), 1)

    @pl.when((pl.program_id(0) == 0) & (step == 0))
    def _():
        key_off = (rk - rq).astype(F32)
        for h in range(heads):
            bias_ref[h] = slopes[h] * key_off

    @pl.when(j == i)
    def _():
        nb = km_ref.shape[1]
        bidx = lax.broadcasted_iota(jnp.int32, (nb, blk), 0)
        for pair in range(heads // 2):
            cols = slice(pair * lanes, (pair + 1) * lanes)
            qp = q_ref[0, :, cols]
            kmp = km_ref[0, :, cols]
            for half in range(2):
                h = 2 * pair + half
                qh = jnp.where(low if half == 0 else ~low, qp, 0.0)
                qm_ref[h] = (qh * scale).astype(BF16)
                gate = _dot_nt(kmp, qh, precision=lax.Precision.HIGHEST)
                chosen = _top_blocks(jnp.where(bidx < i, gate, NEG_INF), bidx, 0)
                bits_ref[h] = jnp.sum(jnp.where(chosen, jnp.left_shift(1, bidx), 0), axis=0, keepdims=True)

    def softmax_step(h, s, first):
        if first:
            s = jnp.where(rk <= rq, s, NEG_INF)
            m_new = jnp.max(s, axis=0, keepdims=True)
            p = jnp.exp(s - m_new)
            l_new = jnp.sum(p, axis=0, keepdims=True)
            alpha = None
        else:
            offset = slopes[h] * ((j - i) * blk).astype(F32)
            picked = (jnp.right_shift(bits_ref[h], j) & 1) == 1
            m_prev = m_ref[h]
            m_new = jnp.where(picked, jnp.maximum(m_prev, jnp.max(s, axis=0, keepdims=True) + offset), m_prev)
            alpha = jnp.exp(m_prev - m_new)
            p = jnp.exp(s - jnp.where(picked, m_new - offset, MASKED_MAX))
            l_new = alpha * l_ref[h] + jnp.sum(p, axis=0, keepdims=True)
        m_ref[h] = m_new
        l_ref[h] = l_new
        return p.astype(BF16), alpha

    def attend(first):
        for h0 in range(0, heads, HEAD_GROUP):
            group = range(h0, h0 + HEAD_GROUP)
            scores = [_dot_nt(k_ref[0, :, (h // 2) * lanes:(h // 2 + 1) * lanes], qm_ref[h]) + bias_ref[h]
                      for h in group]
            probs = [softmax_step(h, s, first) for h, s in zip(group, scores)]
            for h, (p, alpha) in zip(group, probs):
                rows = slice(h * head_dim, (h + 1) * head_dim)
                pv = _dot(vt_ref[0, rows, :], p)
                if first:
                    acc_ref[rows, :] = pv
                else:
                    acc_ref[rows, :] = alpha * acc_ref[rows, :] + pv

    @pl.when(j == i)
    def _():
        attend(True)

    @pl.when(j != i)
    def _():
        attend(False)

    @pl.when((j == i - 1) | (i == 0))
    def _():
        parts = [acc_ref[h * head_dim:(h + 1) * head_dim, :] / l_ref[h] for h in range(heads)]
        out_t = jnp.concatenate(parts, axis=0).astype(BF16)
        y_ref[0] = x_ref[0] + _dot_tn(out_t, wo_ref[...])


def _moba_attn(q32, k16, vt16, km, wo16, x):
    b, s, width = q32.shape
    d = x.shape[-1]
    head_dim = width // MOBA_HEADS
    nt = s // MOBA_BLOCK
    assert nt <= 32, "the chosen (strictly earlier) blocks are kept as bits 0..30 of an int32"
    ti = [i for i in range(nt) for _ in range(i + 1)]
    tj = [j for i in range(nt) for j in [i, *range(i)]]
    q_tile = lambda n, t, ti, tj: (n, ti[t], 0)
    grid_spec = pltpu.PrefetchScalarGridSpec(
        num_scalar_prefetch=2,
        grid=(b, len(ti)),
        in_specs=[pl.BlockSpec((1, MOBA_BLOCK, width), q_tile),
                  pl.BlockSpec((1, MOBA_BLOCK, width), lambda n, t, ti, tj: (n, tj[t], 0)),
                  pl.BlockSpec((1, width, MOBA_BLOCK), lambda n, t, ti, tj: (n, 0, tj[t])),
                  pl.BlockSpec((1, nt, width), lambda n, t, ti, tj: (n, 0, 0)),
                  pl.BlockSpec(wo16.shape, lambda n, t, ti, tj: (0, 0), pipeline_mode=pl.Buffered(1)),
                  pl.BlockSpec((1, MOBA_BLOCK, d), q_tile)],
        out_specs=pl.BlockSpec((1, MOBA_BLOCK, d), q_tile),
        scratch_shapes=[pltpu.VMEM((MOBA_HEADS, MOBA_BLOCK, 2 * head_dim), BF16),
                        pltpu.VMEM((MOBA_HEADS, 1, MOBA_BLOCK), jnp.int32),
                        pltpu.VMEM((MOBA_HEADS, MOBA_BLOCK, MOBA_BLOCK), F32),
                        pltpu.VMEM((MOBA_HEADS, 1, MOBA_BLOCK), F32),
                        pltpu.VMEM((MOBA_HEADS, 1, MOBA_BLOCK), F32),
                        pltpu.VMEM((width, MOBA_BLOCK), F32)])
    return pl.pallas_call(
        functools.partial(_moba_attn_kernel, slopes=_alibi_slopes(MOBA_HEADS), head_dim=head_dim),
        grid_spec=grid_spec,
        out_shape=jax.ShapeDtypeStruct((b, s, d), F32),
        compiler_params=_params("arbitrary", "arbitrary"),
        name="moba_attn",
    )(jnp.asarray(ti, jnp.int32), jnp.asarray(tj, jnp.int32), q32, k16, vt16, km, wo16, x)


def _moba_decode_kernel(pt_ref, q_ref, kvn_ref, slope_ref, p0_ref, p1_ref, o_ref,
                        gate_ref, m_ref, l_ref, acc_ref, *, past_len):
    j = pl.program_id(1)
    heads, head_dim, keys = q_ref.shape[1:]
    scale = head_dim ** -0.5
    q = q_ref[0]
    slope = slope_ref[...]
    lane = lax.broadcasted_iota(jnp.int32, (1, 1, keys), 2)
    raw, scores = [], []
    for half, page_ref in enumerate((p0_ref, p1_ref)):
        dist = (past_len - j * MOBA_BLOCK - half * keys - lane).astype(F32)
        qk = jnp.sum(q * page_ref[0, 0], axis=1, keepdims=True)
        raw.append(qk)
        scores.append(qk * scale - slope * dist)
    m = jnp.maximum(jnp.max(scores[0], axis=-1, keepdims=True), jnp.max(scores[1], axis=-1, keepdims=True))
    p0 = jnp.exp(scores[0] - m)
    p1 = jnp.exp(scores[1] - m)
    @pl.when(j == 0)
    def _():
        gate_ref[...] = jnp.full(gate_ref.shape, NEG_INF, F32)
        m_ref[...] = jnp.zeros(m_ref.shape, F32)
        l_ref[...] = jnp.zeros(l_ref.shape, F32)
        acc_ref[...] = jnp.zeros(acc_ref.shape, F32)

    here = lane == j
    m_ref[...] = jnp.where(here, m, m_ref[...])
    l_ref[...] = jnp.where(here, jnp.sum(p0 + p1, axis=-1, keepdims=True), l_ref[...])
    acc = jnp.sum(p0 * p0_ref[0, 1] + p1 * p1_ref[0, 1], axis=-1, keepdims=True)
    acc_ref[...] = jnp.where(here, acc, acc_ref[...])
    gate = jnp.sum(raw[0] + raw[1], axis=-1, keepdims=True) * (1.0 / MOBA_BLOCK)
    gate_ref[...] = jnp.where(here, gate, gate_ref[...])

    @pl.when(j == pl.num_programs(1) - 1)
    def _():
        chosen = _top_blocks(gate_ref[...], jnp.broadcast_to(lane, gate_ref.shape), 2)
        kn = kvn_ref[0, 0]
        vn = kvn_ref[0, 1]
        s_new = jnp.sum(q * kn, axis=1, keepdims=True)[:, :, 0:1] * scale
        m_all = m_ref[...]
        m_tot = jnp.maximum(jnp.max(jnp.where(chosen, m_all, NEG_INF), axis=-1, keepdims=True), s_new)
        w = jnp.where(chosen, jnp.exp(m_all - m_tot), 0.0)
        e_new = jnp.exp(s_new - m_tot)
        den = jnp.sum(w * l_ref[...], axis=-1, keepdims=True) + e_new
        o_ref[0] = (jnp.sum(w * acc_ref[...], axis=-1, keepdims=True) + e_new * vn) / den


def _moba_decode(q, kv_new, cache, page_table):
    n, heads, head_dim = q.shape
    n_pages = page_table.shape[1]
    per_blk = MOBA_BLOCK // PAGE_SIZE
    assert cache.shape[1] == PAGE_SIZE and per_blk == 2 and n_pages % per_blk == 0
    nb = n_pages // per_blk
    assert nb <= PAGE_SIZE, "per-block statistics are kept one block per lane"
    pages_t = jnp.transpose(cache, (0, 2, 3, 4, 1))
    q_b = jnp.broadcast_to(q[..., None], (n, heads, head_dim, PAGE_SIZE))
    slopes = jnp.broadcast_to(jnp.asarray(_alibi_slopes(heads), F32)[:, None, None], (heads, 1, PAGE_SIZE))
    row4 = lambda i, j, pt: (i, 0, 0, 0)
    row5 = lambda i, j, pt: (i, 0, 0, 0, 0)
    page = (1, 2, heads, head_dim, PAGE_SIZE)
    grid_spec = pltpu.PrefetchScalarGridSpec(
        num_scalar_prefetch=1,
        grid=(n, nb),
        in_specs=[pl.BlockSpec((1, heads, head_dim, PAGE_SIZE), row4),
                  pl.BlockSpec((1, 2, heads, head_dim, 1), row5),
                  pl.BlockSpec(slopes.shape, lambda i, j, pt: (0, 0, 0)),
                  pl.BlockSpec(page, lambda i, j, pt: (pt[i, 2 * j], 0, 0, 0, 0)),
                  pl.BlockSpec(page, lambda i, j, pt: (pt[i, 2 * j + 1], 0, 0, 0, 0))],
        out_specs=pl.BlockSpec((1, heads, head_dim, 1), row4),
        scratch_shapes=[pltpu.VMEM((heads, 1, PAGE_SIZE), F32),
                        pltpu.VMEM((heads, 1, PAGE_SIZE), F32),
                        pltpu.VMEM((heads, 1, PAGE_SIZE), F32),
                        pltpu.VMEM((heads, head_dim, PAGE_SIZE), F32)])
    o = pl.pallas_call(
        functools.partial(_moba_decode_kernel, past_len=n_pages * PAGE_SIZE),
        grid_spec=grid_spec,
        out_shape=jax.ShapeDtypeStruct((n, heads, head_dim, 1), F32),
        compiler_params=_params("arbitrary", "arbitrary"),
        name="moba_decode",
    )(page_table, q_b, kv_new[..., None], slopes, pages_t, pages_t)
    return o.reshape(n, heads, head_dim)


def _moba_layer(yp, ys, cache, page_table, g_mix, w_qkv, q_gain, k_gain, w_o):
    b, s, d = yp.shape
    n = ys.shape[0]
    width = w_qkv.shape[1] // 3
    head_dim = width // MOBA_HEADS
    w16 = w_qkv.astype(BF16)
    wo16 = w_o.astype(BF16)
    gains = jnp.stack([_tiled_gain(q_gain, MOBA_HEADS), _tiled_gain(k_gain, MOBA_HEADS), jnp.ones((width,), F32)])
    gains = gains.reshape(3, 1, width)
    q32, kv32, k16, vt16, km = _moba_proj(yp, g_mix, w16, gains, prompt=True)
    yp_new = _moba_attn(q32, k16, vt16, km.reshape(b, s // MOBA_BLOCK, width), wo16, yp)
    rows_p = kv32.reshape(b, s, 2, MOBA_HEADS, head_dim)
    qs32, kvs32 = _moba_proj(ys[None], g_mix, w16, gains, prompt=False)
    att_s = _moba_decode(qs32.reshape(n, MOBA_HEADS, head_dim), kvs32.reshape(n, 2, MOBA_HEADS, head_dim),
                         cache, page_table)
    ys_new = _out_proj(att_s.reshape(n, width), wo16, ys)
    rows_s = kvs32.reshape(n, 1, 2, MOBA_HEADS, head_dim)
    return yp_new, ys_new, rows_p, rows_s


def kernel(x_prompt, x_sample, state_pool, cache_dil0, cache_dil1, cache_dil2, state_conv, cache_moba, page_table,
           norm_mix, norm_ffn, pool_w, pool_scale, dil_w_qkv, dil_q_gain, dil_k_gain, dil_w_o,
           moba_w_qkv, moba_q_gain, moba_k_gain, moba_w_o, ffn_w_in, ffn_conv_w, ffn_conv_b, ffn_w_out):
    assert x_sample.shape[1] == 1, "the sample group decodes one token per sequence"
    depth = norm_mix.shape[0]
    yp, ys = x_prompt, x_sample[:, 0]
    pool_p, pool_s, moba_p, moba_s, conv_p, conv_s = [], [], [], [], [], []
    dil_p = [[] for _ in DIL_PATTERNS]
    dil_s = [[] for _ in DIL_PATTERNS]
    for layer in range(depth):
        kind, j = layer % N_MIXERS, layer // N_MIXERS
        g_mix = norm_mix[layer][None]
        if kind == 0:
            w16 = pool_w[j].astype(BF16)
            scale = pool_scale[j][None]
            yp, tail = _pool_prompt(yp, g_mix, w16, scale)
            ys, hs = _pool_sample(ys, state_pool[j].transpose(1, 0, 2), g_mix, w16, scale)
            pool_p.append(tail[:, POOL_CARRY - POOL_HIST:])
            pool_s.append(jnp.concatenate([state_pool[j][:, 1:], hs[:, None]], axis=1))
        elif kind == 1:
            caches = [c[j] for c in (cache_dil0, cache_dil1, cache_dil2)]
            yp, ys, rows_p, rows_s = _dil_layer(yp, ys, caches, g_mix, dil_w_qkv[j], dil_q_gain[j], dil_k_gain[j],
                                                dil_w_o[j])
            for g in range(len(DIL_PATTERNS)):
                dil_p[g].append(rows_p[g])
                dil_s[g].append(rows_s[g])
        else:
            yp, ys, rows_p, rows_s = _moba_layer(yp, ys, cache_moba[j], page_table, g_mix, moba_w_qkv[j],
                                                 moba_q_gain[j], moba_k_gain[j], moba_w_o[j])
            moba_p.append(rows_p)
            moba_s.append(rows_s)
        g_ffn = norm_ffn[layer][None]
        win16 = ffn_w_in[layer].astype(BF16)
        wout16 = ffn_w_out[layer].astype(BF16)
        conv_w, conv_b = ffn_conv_w[layer], ffn_conv_b[layer][None]
        yp, tail = _ffn_prompt(yp, g_ffn, win16, conv_w, conv_b, wout16)
        ys, ua, ug = _ffn_sample(ys, g_ffn, win16, state_conv[layer].transpose(1, 0, 2), conv_w, conv_b, wout16)
        conv_p.append(tail[:, CONV_CARRY - CONV_HIST:])
        u_new = jnp.concatenate([ua, ug], axis=-1)[:, None]
        conv_s.append(jnp.concatenate([state_conv[layer][:, 1:], u_new], axis=1))
    stack = lambda xs: jnp.stack(xs, axis=0)
    return (yp, ys[:, None], stack(pool_p), stack(pool_s),
            stack(dil_p[0]), stack(dil_p[1]), stack(dil_p[2]),
            stack(dil_s[0]), stack(dil_s[1]), stack(dil_s[2]),
            stack(moba_p), stack(moba_s), stack(conv_p), stack(conv_s))
```

```python
import functools

import jax
import jax.numpy as jnp
from jax import lax
from jax.experimental import pallas as pl
from jax.experimental.pallas import tpu as pltpu

F32 = jnp.float32
BF16 = jnp.bfloat16

N_MIXERS = 3
POOL_WINDOWS = (2, 4, 8, 16)
POOL_HIST = max(POOL_WINDOWS) - 1
POOL_CARRY = 16
DIL_PATTERNS = ((128, 1), (512, 4), (2048, 16))
DIL_BAND = 128
DIL_HEADS = 8
LSE_LANES = 16
MOBA_BLOCK = 256
MOBA_TOPK = 3
MOBA_HEADS = 16
HEAD_GROUP = 16
DECODE_BLOCKS = 2
PAGE_SIZE = 128
CONV_WIDTH = 3
CONV_HIST = CONV_WIDTH - 1
CONV_CARRY = 8
RMS_EPS = 1e-6
LOG2E = 1.4426950408889634
ALIBI_PIECES = 3
NEG_INF = float("-inf")
MASKED_MAX = 1e30
VMEM_LIMIT = 56 * 1024 * 1024


def _params(*sem):
    return pltpu.CompilerParams(dimension_semantics=sem, vmem_limit_bytes=VMEM_LIMIT)


def _resident(shape):
    zeros = (0,) * len(shape)
    return pl.BlockSpec(shape, lambda *_: zeros, pipeline_mode=pl.Buffered(1))


def _rms(x, g):
    return x * lax.rsqrt(jnp.mean(x * x, axis=-1, keepdims=True) + RMS_EPS) * g


def _dot(a, b):
    return jnp.dot(a, b, preferred_element_type=F32)


def _dot_nt(a, b, precision=None):
    return lax.dot_general(a, b, (((1,), (1,)), ((), ())), preferred_element_type=F32, precision=precision)


def _as_dot_operand(x):
    return x.astype(BF16).astype(F32)


def _dot_tn(a, b):
    return lax.dot_general(a, b, (((0,), (0,)), ((), ())), preferred_element_type=F32)


def _alibi_slopes(n):
    return [2.0 ** (-8.0 * i / n) for i in range(1, n + 1)]


def _pool_prompt_kernel(x_ref, g_ref, w_ref, sc_ref, y_ref, tail_ref, ext_ref, *, tile):
    t = pl.program_id(1)
    d_model = x_ref.shape[-1]
    gdim = d_model // len(POOL_WINDOWS)

    @pl.when(t == 0)
    def _():
        ext_ref[0:POOL_CARRY, :] = jnp.zeros((POOL_CARRY, d_model), F32)

    x = x_ref[0]
    ext_ref[POOL_CARRY:POOL_CARRY + tile, :] = _rms(x, g_ref[...])
    row = t * tile + lax.broadcasted_iota(jnp.int32, (tile, 1), 0)
    for gi, win in enumerate(POOL_WINDOWS):
        cols = slice(gi * gdim, (gi + 1) * gdim)
        cur = ext_ref[POOL_CARRY:POOL_CARRY + tile, cols]
        acc = cur
        for j in range(1, win):
            acc = acc + ext_ref[POOL_CARRY - j:POOL_CARRY - j + tile, cols]
        cnt = jnp.minimum(row + 1, win).astype(F32)
        d = acc / cnt - cur
        y = _dot(d.astype(BF16), w_ref[gi])
        y_ref[0, :, cols] = x[:, cols] + y * sc_ref[:, cols]
    last = ext_ref[tile:tile + POOL_CARRY, :]
    ext_ref[0:POOL_CARRY, :] = last

    @pl.when(t == pl.num_programs(1) - 1)
    def _():
        tail_ref[0] = last


def _pool_prompt(x, g, w16, sc, tile=512):
    b, s, d = x.shape
    tile = min(tile, s)
    return pl.pallas_call(
        functools.partial(_pool_prompt_kernel, tile=tile),
        grid=(b, s // tile),
        in_specs=[pl.BlockSpec((1, tile, d), lambda i, t: (i, t, 0)),
                  _resident((1, d)), _resident(w16.shape), _resident((1, d))],
        out_specs=[pl.BlockSpec((1, tile, d), lambda i, t: (i, t, 0)),
                   pl.BlockSpec((1, POOL_CARRY, d), lambda i, t: (i, 0, 0))],
        out_shape=[jax.ShapeDtypeStruct((b, s, d), F32), jax.ShapeDtypeStruct((b, POOL_CARRY, d), F32)],
        scratch_shapes=[pltpu.VMEM((POOL_CARRY + tile, d), F32)],
        compiler_params=_params("arbitrary", "arbitrary"),
        name="pool_prompt",
    )(x, g, w16, sc)


def _pool_sample_kernel(x_ref, st_ref, g_ref, w_ref, sc_ref, y_ref, h_ref):
    d_model = x_ref.shape[-1]
    gdim = d_model // len(POOL_WINDOWS)
    x = x_ref[...]
    h = _rms(x, g_ref[...])
    h_ref[...] = h
    for gi, win in enumerate(POOL_WINDOWS):
        cols = slice(gi * gdim, (gi + 1) * gdim)
        cur = h[:, cols]
        acc = cur
        for j in range(1, win):
            acc = acc + st_ref[POOL_HIST - j, :, cols]
        d = acc / float(win) - cur
        y = _dot(d.astype(BF16), w_ref[gi])
        y_ref[:, cols] = x[:, cols] + y * sc_ref[:, cols]


def _pool_sample(x, state_t, g, w16, sc):
    n, d = x.shape
    return pl.pallas_call(
        _pool_sample_kernel,
        out_shape=[jax.ShapeDtypeStruct((n, d), F32), jax.ShapeDtypeStruct((n, d), F32)],
        compiler_params=pltpu.CompilerParams(vmem_limit_bytes=VMEM_LIMIT),
        name="pool_sample",
    )(x, state_t, g, w16, sc)


def _silu(g):
    return g * (1.0 / (1.0 + jnp.exp(-g)))


def _ffn_prompt_kernel(y_ref, g_ref, win_ref, cw_ref, cb_ref, wout_ref, o_ref, tail_ref,
                       u_ref, carry_ref, acc_ref, *, tile, chunk):
    t = pl.program_id(1)
    hidden = wout_ref.shape[0]

    @pl.when(t == 0)
    def _():
        carry_ref[...] = jnp.zeros(carry_ref.shape, F32)

    y = y_ref[0]
    h = _rms(y, g_ref[...]).astype(BF16)
    n_chunks = hidden // chunk

    def up_project(c):
        us = []
        for part in range(2):
            cols = slice(part * hidden + c * chunk, part * hidden + (c + 1) * chunk)
            u = _dot(h, win_ref[:, cols])
            u_ref[c % 2, part, 0:CONV_CARRY, :] = carry_ref[:, cols]
            u_ref[c % 2, part, CONV_CARRY:CONV_CARRY + tile, :] = u
            carry_ref[:, cols] = u[tile - CONV_CARRY:tile, :]
            us.append(u)
        return us

    def conv_gate(c, us):
        conv = []
        for part in range(2):
            cols = slice(part * hidden + c * chunk, part * hidden + (c + 1) * chunk)
            cv = cb_ref[:, cols] + cw_ref[2:3, cols] * us[part]
            cv = cv + cw_ref[1:2, cols] * u_ref[c % 2, part, CONV_CARRY - 1:CONV_CARRY - 1 + tile, :]
            cv = cv + cw_ref[0:1, cols] * u_ref[c % 2, part, CONV_CARRY - 2:CONV_CARRY - 2 + tile, :]
            conv.append(cv)
        return (_silu(conv[1]) * conv[0]).astype(BF16)

    us_next = up_project(0)
    for c in range(n_chunks):
        us = us_next
        if c + 1 < n_chunks:
            us_next = up_project(c + 1)
        contrib = _dot(conv_gate(c, us), wout_ref[c * chunk:(c + 1) * chunk, :])
        if c == 0:
            acc_ref[...] = contrib
        else:
            acc_ref[...] += contrib
    o_ref[0] = y + acc_ref[...]

    @pl.when(t == pl.num_programs(1) - 1)
    def _():
        tail_ref[0] = carry_ref[...]


def _ffn_prompt(y, g, win16, cw, cb, wout16, tile=512, chunk=256):
    b, s, d = y.shape
    hidden = wout16.shape[0]
    tile = min(tile, s)
    return pl.pallas_call(
        functools.partial(_ffn_prompt_kernel, tile=tile, chunk=chunk),
        grid=(b, s // tile),
        in_specs=[pl.BlockSpec((1, tile, d), lambda i, t: (i, t, 0)),
                  _resident((1, d)), _resident(win16.shape), _resident(cw.shape), _resident(cb.shape),
                  _resident(wout16.shape)],
        out_specs=[pl.BlockSpec((1, tile, d), lambda i, t: (i, t, 0)),
                   pl.BlockSpec((1, CONV_CARRY, 2 * hidden), lambda i, t: (i, 0, 0))],
        out_shape=[jax.ShapeDtypeStruct((b, s, d), F32),
                   jax.ShapeDtypeStruct((b, CONV_CARRY, 2 * hidden), F32)],
        scratch_shapes=[pltpu.VMEM((2, 2, CONV_CARRY + tile, chunk), F32),
                        pltpu.VMEM((CONV_CARRY, 2 * hidden), F32),
                        pltpu.VMEM((tile, d), F32)],
        compiler_params=_params("arbitrary", "arbitrary"),
        name="ffn_prompt",
    )(y, g, win16, cw, cb, wout16)


def _ffn_sample_kernel(y_ref, g_ref, wa_ref, wg_ref, ha_ref, hg_ref, cwa_ref, cwg_ref, cba_ref, cbg_ref,
                       wout_ref, o_ref, ua_ref, ug_ref):
    c = pl.program_id(0)
    y = y_ref[...]
    h = _rms(y, g_ref[...]).astype(BF16)

    def conv(w_ref, hist_ref, cw_ref, cb_ref, u_out_ref):
        u = _dot(h, w_ref[...])
        u_out_ref[...] = u
        return cb_ref[...] + cw_ref[0:1, :] * hist_ref[0] + cw_ref[1:2, :] * hist_ref[1] + cw_ref[2:3, :] * u

    ca = conv(wa_ref, ha_ref, cwa_ref, cba_ref, ua_ref)
    cg = conv(wg_ref, hg_ref, cwg_ref, cbg_ref, ug_ref)
    contrib = _dot((_silu(cg) * ca).astype(BF16), wout_ref[...])

    @pl.when(c == 0)
    def _():
        o_ref[...] = y + contrib

    @pl.when(c != 0)
    def _():
        o_ref[...] += contrib


def _ffn_sample(y, g, win16, hist_t, cw, cb, wout16, chunk=256):
    n, d = y.shape
    hidden = wout16.shape[0]
    nc = hidden // chunk
    a_col = lambda c: (0, c)
    g_col = lambda c: (0, c + nc)
    return pl.pallas_call(
        _ffn_sample_kernel,
        grid=(nc,),
        in_specs=[_resident((n, d)), _resident((1, d)),
                  pl.BlockSpec((d, chunk), a_col), pl.BlockSpec((d, chunk), g_col),
                  pl.BlockSpec((CONV_HIST, n, chunk), lambda c: (0, 0, c)),
                  pl.BlockSpec((CONV_HIST, n, chunk), lambda c: (0, 0, c + nc)),
                  pl.BlockSpec((CONV_WIDTH, chunk), a_col), pl.BlockSpec((CONV_WIDTH, chunk), g_col),
                  pl.BlockSpec((1, chunk), a_col), pl.BlockSpec((1, chunk), g_col),
                  pl.BlockSpec((chunk, d), lambda c: (c, 0))],
        out_specs=[pl.BlockSpec((n, d), lambda c: (0, 0)),
                   pl.BlockSpec((n, chunk), a_col), pl.BlockSpec((n, chunk), a_col)],
        out_shape=[jax.ShapeDtypeStruct((n, d), F32), jax.ShapeDtypeStruct((n, hidden), F32),
                   jax.ShapeDtypeStruct((n, hidden), F32)],
        compiler_params=_params("arbitrary"),
        name="ffn_sample",
    )(y, g, win16, win16, hist_t, hist_t, cw, cw, cb, cb, wout16)


def _out_proj_kernel(a_ref, w_ref, x_ref, y_ref):
    y_ref[...] = x_ref[...] + _dot(a_ref[...].astype(BF16), w_ref[...])


def _out_proj(a, w16, x, tile=512):
    n, d = x.shape
    tile = min(tile, n)
    return pl.pallas_call(
        _out_proj_kernel,
        grid=(n // tile,),
        in_specs=[pl.BlockSpec((tile, a.shape[1]), lambda t: (t, 0)), _resident(w16.shape),
                  pl.BlockSpec((tile, d), lambda t: (t, 0))],
        out_specs=pl.BlockSpec((tile, d), lambda t: (t, 0)),
        out_shape=jax.ShapeDtypeStruct((n, d), F32),
        compiler_params=_params("arbitrary"),
        name="out_proj",
    )(a, w16, x)


def _out_proj_heads_kernel(a_ref, w_ref, x_ref, y_ref):
    a = jnp.concatenate([a_ref[0, h] for h in range(a_ref.shape[1])], axis=1)
    y_ref[0] = x_ref[0] + _dot(a.astype(BF16), w_ref[...])


def _out_proj_heads(a, w16, x, tile=512):
    b, heads, s, head_dim = a.shape
    d = x.shape[-1]
    tile = min(tile, s)
    return pl.pallas_call(
        _out_proj_heads_kernel,
        grid=(b, s // tile),
        in_specs=[pl.BlockSpec((1, heads, tile, head_dim), lambda i, t: (i, 0, t, 0)), _resident(w16.shape),
                  pl.BlockSpec((1, tile, d), lambda i, t: (i, t, 0))],
        out_specs=pl.BlockSpec((1, tile, d), lambda i, t: (i, t, 0)),
        out_shape=jax.ShapeDtypeStruct((b, s, d), F32),
        compiler_params=_params("arbitrary", "arbitrary"),
        name="out_proj_heads",
    )(a, w16, x)


def _head_norm(r, gain, head_dim):
    lanes = 128
    low = lax.broadcasted_iota(jnp.int32, (1, lanes), 1) < head_dim
    cols = []
    for v in range(r.shape[1] // lanes):
        seg = r[:, v * lanes:(v + 1) * lanes]
        sq = seg * seg
        if head_dim == lanes:
            ms = jnp.mean(sq, axis=-1, keepdims=True)
        else:
            lo = jnp.sum(jnp.where(low, sq, 0.0), axis=-1, keepdims=True)
            hi = jnp.sum(jnp.where(low, 0.0, sq), axis=-1, keepdims=True)
            ms = jnp.where(low, lo, hi) * (1.0 / head_dim)
        cols.append(seg * lax.rsqrt(ms + RMS_EPS))
    return jnp.concatenate(cols, axis=1) * gain


def _tiled_gain(gain, heads):
    return jnp.tile(gain.astype(F32), heads)


def _dil_proj_kernel(x_ref, g_ref, w_ref, gain_ref, o0_ref, o1_ref, o2_ref, kv32_ref, h_ref, r_ref,
                     *, dils, head_dim):
    j = pl.program_id(2)
    g = j // 3
    c = j % 3
    chunks, tile, lanes = r_ref.shape

    @pl.when(j == 0)
    def _():
        h_ref[...] = _rms(x_ref[0], g_ref[...]).astype(BF16)

    r = _dot(h_ref[...], w_ref[...])

    def emit(val):
        @pl.when(c > 0)
        def _():
            kv32_ref[0] = val

        for gi, (o_ref, dil) in enumerate(zip((o0_ref, o1_ref, o2_ref), dils)):
            @pl.when(g == gi)
            def _(o_ref=o_ref, dil=dil):
                if dil == 1:
                    o_ref[0, 0] = val.astype(BF16)
                    return
                for ch in range(chunks):
                    r_ref[ch] = val[:, ch * lanes:(ch + 1) * lanes]
                for res in range(dil):
                    for ch in range(chunks):
                        o_ref[0, res, :, ch * lanes:(ch + 1) * lanes] = (
                            r_ref[ch, pl.ds(res, tile // dil, stride=dil), :].astype(BF16))

    @pl.when(c < 2)
    def _():
        emit(_head_norm(r, gain_ref[0], head_dim))

    @pl.when(c == 2)
    def _():
        emit(r)


def _dil_proj(x, g, w16, gains, tail_rows, dils, tile=1024):
    b, s, d = x.shape
    groups = len(dils)
    width = w16.shape[1] // (3 * groups)
    tile = min(tile, s, tail_rows)
    t0 = (s - tail_rows) // tile

    def kv_map(i, t, j):
        col = (j // 3) * 2 + jnp.maximum(j % 3 - 1, 0)
        return i, jnp.maximum(t - t0, 0), jnp.where(t >= t0, col, 0)

    def group_spec(gi, dil):
        return pl.BlockSpec((1, dil, tile // dil, width),
                            lambda i, t, j: (i, 0, t, jnp.clip(j - 3 * gi, 0, 2)))

    return pl.pallas_call(
        functools.partial(_dil_proj_kernel, dils=tuple(dils), head_dim=width // DIL_HEADS),
        grid=(b, s // tile, 3 * groups),
        in_specs=[pl.BlockSpec((1, tile, d), lambda i, t, j: (i, t, 0)), _resident((1, d)),
                  pl.BlockSpec((d, width), lambda i, t, j: (0, j)),
                  pl.BlockSpec((1, 1, width), lambda i, t, j: (j, 0, 0))],
        out_specs=[group_spec(gi, dil) for gi, dil in enumerate(dils)]
        + [pl.BlockSpec((1, tile, width), kv_map)],
        out_shape=[jax.ShapeDtypeStruct((b, dil, s // dil, 3 * width), BF16) for dil in dils]
        + [jax.ShapeDtypeStruct((b, tail_rows, groups * 2 * width), F32)],
        scratch_shapes=[pltpu.VMEM((tile, d), BF16), pltpu.VMEM((width // 128, tile, 128), F32)],
        compiler_params=_params("arbitrary", "arbitrary", "arbitrary"),
        name="dil_proj",
    )(x, g, w16, gains)


def _moba_proj_kernel(x_ref, g_ref, w_ref, gain_ref, *refs, head_dim, prompt):
    if prompt:
        wvt_ref, q32_ref, kv32_ref, k16_ref, vt16_ref, km_ref, h_ref = refs
    else:
        q32_ref, kv32_ref, h_ref = refs
    j = pl.program_id(2)

    @pl.when(j == 0)
    def _():
        h_ref[...] = _rms(x_ref[0], g_ref[...]).astype(BF16)

    r = _dot(h_ref[...], w_ref[...])

    @pl.when(j == 0)
    def _():
        q32_ref[0] = _head_norm(r, gain_ref[0], head_dim)

    @pl.when(j == 1)
    def _():
        rn = _head_norm(r, gain_ref[0], head_dim)
        kv32_ref[0] = rn
        if prompt:
            lanes = 2 * head_dim
            lane = lax.broadcasted_iota(jnp.int32, (1, lanes), 1)
            in_block = (lax.broadcasted_iota(jnp.int32, (rn.shape[0], 1), 0) % MOBA_BLOCK).astype(F32)
            for h, slope in enumerate(_alibi_slopes(rn.shape[1] // head_dim)):
                kp = rn[:, (h // 2) * lanes:(h // 2 + 1) * lanes]
                own = (lane < head_dim) if h % 2 == 0 else (lane >= head_dim)
                base = head_dim if h % 2 == 0 else 0
                rest = in_block * (slope * LOG2E)
                term = jnp.zeros((1, lanes), F32)
                for piece in range(ALIBI_PIECES):
                    part = rest if piece == ALIBI_PIECES - 1 else rest.astype(BF16).astype(F32)
                    term = jnp.where(lane == base + piece, part, term)
                    rest = rest - part
                k16_ref[0, :, h * lanes:(h + 1) * lanes] = jnp.where(own, kp, term).astype(BF16)
            for blk in range(rn.shape[0] // MOBA_BLOCK):
                rows = rn[blk * MOBA_BLOCK:(blk + 1) * MOBA_BLOCK, :]
                km_ref[0, blk] = jnp.mean(rows, axis=0, keepdims=True)

    @pl.when(j == 2)
    def _():
        kv32_ref[0] = r
        if prompt:
            vt16_ref[0] = _dot_nt(wvt_ref[...], h_ref[...]).astype(BF16)


def _moba_proj(x, g, w16, gains, prompt, tile=1024):
    b, s, d = x.shape
    width = w16.shape[1] // 3
    tile = min(tile, s)
    in_specs = [pl.BlockSpec((1, tile, d), lambda i, t, j: (i, t, 0)), _resident((1, d)),
                pl.BlockSpec((d, width), lambda i, t, j: (0, j)),
                pl.BlockSpec((1, 1, width), lambda i, t, j: (j, 0, 0))]
    args = [x, g, w16, gains]
    out_specs = [pl.BlockSpec((1, tile, width), lambda i, t, j: (i, t, 0)),
                 pl.BlockSpec((1, tile, width), lambda i, t, j: (i, t, jnp.maximum(j - 1, 0)))]
    out_shape = [jax.ShapeDtypeStruct((b, s, width), F32), jax.ShapeDtypeStruct((b, s, 2 * width), F32)]
    if prompt:
        in_specs.append(_resident((width, d)))
        args.append(w16[:, 2 * width:].T)
        out_specs += [pl.BlockSpec((1, tile, 2 * width), lambda i, t, j: (i, t, 0)),
                      pl.BlockSpec((1, width, tile), lambda i, t, j: (i, 0, t)),
                      pl.BlockSpec((1, tile // MOBA_BLOCK, 1, width), lambda i, t, j: (i, t, 0, 0))]
        out_shape += [jax.ShapeDtypeStruct((b, s, 2 * width), BF16), jax.ShapeDtypeStruct((b, width, s), BF16),
                      jax.ShapeDtypeStruct((b, s // MOBA_BLOCK, 1, width), F32)]
    return pl.pallas_call(
        functools.partial(_moba_proj_kernel, head_dim=width // MOBA_HEADS, prompt=prompt),
        grid=(b, s // tile, 3),
        in_specs=in_specs,
        out_specs=out_specs,
        out_shape=out_shape,
        scratch_shapes=[pltpu.VMEM((tile, d), BF16)],
        compiler_params=_params("arbitrary", "arbitrary", "arbitrary"),
        name="moba_proj",
    )(*args)


def _dil_attn_kernel(q_ref, kp_ref, ko_ref, vp_ref, vo_ref, *refs, dil, slopes, head_dim, chained, last):
    refs = list(refs)
    oin_ref, lin_ref = (refs.pop(0), refs.pop(0)) if chained else (None, None)
    o_ref = refs.pop(0)
    l_ref = None if last else refs.pop(0)
    n = pl.program_id(1)
    r = pl.program_id(2)
    band = DIL_BAND
    width = len(slopes) * head_dim
    scale = head_dim ** -0.5
    rows = slice(None) if dil == 1 else pl.ds(r, band, stride=dil)
    qi = lax.broadcasted_iota(jnp.int32, (band, band), 0)
    kj = lax.broadcasted_iota(jnp.int32, (band, band), 1)
    nd_own = jnp.where(kj <= qi, ((kj - qi) * dil).astype(F32), NEG_INF)
    nd_prev = jnp.where((kj >= qi) & (n > 0), ((kj - qi - band) * dil).astype(F32), NEG_INF)
    lane_head = lax.broadcasted_iota(jnp.int32, (1, band), 1) // LSE_LANES
    if chained:
        l_in = lin_ref[0, rows, :]
    lse_out = jnp.zeros((band, band), F32)
    heads = range(len(slopes))
    head_cols = [slice(h * head_dim, (h + 1) * head_dim) for h in heads]
    scores = []
    for h in heads:
        q = q_ref[0, 0, :, head_cols[h]]
        scores.append((_dot_nt(q, ko_ref[0, 0, :, head_cols[h]]) * scale + slopes[h] * nd_own,
                       _dot_nt(q, kp_ref[0, 0, :, head_cols[h]]) * scale + slopes[h] * nd_prev))
    probs = []
    for h, (s_o, s_p) in zip(heads, scores):
        m = jnp.maximum(jnp.max(s_o, axis=-1, keepdims=True), jnp.max(s_p, axis=-1, keepdims=True))
        if chained:
            lse_prev = l_in[:, h * LSE_LANES:h * LSE_LANES + 1]
            m = jnp.maximum(m, lse_prev)
        p_o = jnp.exp(s_o - m)
        p_p = jnp.exp(s_p - m)
        l = jnp.sum(p_o, axis=-1, keepdims=True) + jnp.sum(p_p, axis=-1, keepdims=True)
        w_prev = None
        if chained:
            w_prev = jnp.exp(lse_prev - m)
            l = l + w_prev
        if not last:
            lse_out = jnp.where(lane_head == h, m + jnp.log(l), lse_out)
        probs.append((p_o.astype(BF16), p_p.astype(BF16), l, w_prev))
    for h, (p_o, p_p, l, w_prev) in zip(heads, probs):
        acc = _dot(p_o, vo_ref[0, 0, :, head_cols[h]]) + _dot(p_p, vp_ref[0, 0, :, head_cols[h]])
        if chained:
            acc = acc + w_prev * oin_ref[0, h, rows, :]
        o_ref[0, h, rows, :] = acc / l
    if not last:
        l_ref[0, rows, :] = lse_out


def _dil_attn(qkv, g, state):
    b, dil, n_sub, ncol = qkv.shape
    width = ncol // 3
    head_dim = width // DIL_HEADS
    s = n_sub * dil
    n_blk = n_sub // DIL_BAND
    groups = len(DIL_PATTERNS)
    last = g == groups - 1
    slopes = _alibi_slopes(groups * DIL_HEADS)[g * DIL_HEADS:(g + 1) * DIL_HEADS]

    def spec(c, prev):
        return pl.BlockSpec((1, 1, DIL_BAND, width),
                            lambda i, n, r: (i, r, (jnp.maximum(n - 1, 0) if prev else n), c))

    rows = DIL_BAND * dil
    o_spec = pl.BlockSpec((1, DIL_HEADS, rows, head_dim), lambda i, n, r: (i, 0, n, 0))
    l_spec = pl.BlockSpec((1, rows, DIL_BAND), lambda i, n, r: (i, n, 0))
    in_specs = [spec(0, False), spec(1, True), spec(1, False), spec(2, True), spec(2, False)]
    args = [qkv] * 5
    if state is not None:
        in_specs += [o_spec, l_spec]
        args += list(state)
    out_specs = [o_spec] if last else [o_spec, l_spec]
    out_shape = [jax.ShapeDtypeStruct((b, DIL_HEADS, s, head_dim), F32)]
    if not last:
        out_shape.append(jax.ShapeDtypeStruct((b, s, DIL_BAND), F32))
    return pl.pallas_call(
        functools.partial(_dil_attn_kernel, dil=dil, slopes=slopes, head_dim=head_dim,
                          chained=state is not None, last=last),
        grid=(b, n_blk, dil),
        in_specs=in_specs,
        out_specs=out_specs,
        out_shape=out_shape,
        compiler_params=_params("arbitrary", "arbitrary", "arbitrary"),
        name=f"dil_attn{g}",
    )(*args)


def _dil_decode_kernel(q_ref, kvn_ref, slope_ref, c0_ref, c1_ref, c2_ref, o_ref, *, head_dim):
    scale = head_dim ** -0.5
    steps = (DIL_BAND - lax.broadcasted_iota(jnp.int32, (DIL_BAND, 1, 1), 0)).astype(F32)
    outs, lses = [], []
    for g, c_ref in enumerate((c0_ref, c1_ref, c2_ref)):
        dil = DIL_PATTERNS[g][1]
        q = q_ref[0, g]
        kn = _as_dot_operand(kvn_ref[0, 2 * g])
        vn = _as_dot_operand(kvn_ref[0, 2 * g + 1])
        kc = _as_dot_operand(c_ref[0, :, 0, 0])
        vc = _as_dot_operand(c_ref[0, :, 0, 1])
        slope = slope_ref[g][:, 0:1]
        s = jnp.sum(kc * q[None], axis=-1, keepdims=True) * scale - slope[None] * (steps * float(dil))
        s_new = jnp.sum(kn * q, axis=-1, keepdims=True) * scale
        m = jnp.maximum(jnp.max(s, axis=0), s_new)
        p = jnp.exp(s - m[None])
        p_new = jnp.exp(s_new - m)
        l = jnp.sum(p, axis=0) + p_new
        acc = jnp.sum(_as_dot_operand(p) * vc, axis=0) + _as_dot_operand(p_new) * vn
        outs.append(acc / l)
        lses.append(m + jnp.log(l))
    m = jnp.maximum(jnp.maximum(lses[0], lses[1]), lses[2])
    es = [jnp.exp(l - m) for l in lses]
    o_ref[0] = (es[0] * outs[0] + es[1] * outs[1] + es[2] * outs[2]) / (es[0] + es[1] + es[2])


def _dil_decode(q, kv_new, caches):
    n, _, heads, head_dim = q.shape
    views, specs = [], []
    for g, (win, dil) in enumerate(DIL_PATTERNS):
        assert caches[g].shape[1] == win, "the window buffers must be full"
        views.append(caches[g].reshape(n, DIL_BAND, dil, 2, heads, head_dim))
        specs.append(pl.BlockSpec((1, DIL_BAND, 1, 2, heads, head_dim), lambda i: (i, 0, 0, 0, 0, 0)))
    slopes = jnp.asarray(_alibi_slopes(len(DIL_PATTERNS) * heads), F32).reshape(len(DIL_PATTERNS), heads, 1)
    slopes = jnp.broadcast_to(slopes, (len(DIL_PATTERNS), heads, head_dim))
    return pl.pallas_call(
        functools.partial(_dil_decode_kernel, head_dim=head_dim),
        grid=(n,),
        in_specs=[pl.BlockSpec((1,) + q.shape[1:], lambda i: (i, 0, 0, 0)),
                  pl.BlockSpec((1,) + kv_new.shape[1:], lambda i: (i, 0, 0, 0)),
                  _resident(slopes.shape)] + specs,
        out_specs=pl.BlockSpec((1, heads, head_dim), lambda i: (i, 0, 0)),
        out_shape=jax.ShapeDtypeStruct((n, heads, head_dim), F32),
        compiler_params=_params("arbitrary"),
        name="dil_decode",
    )(q, kv_new, slopes, *views)


def _dil_layer(yp, ys, caches, g_mix, w_qkv, q_gain, k_gain, w_o):
    b, s, d = yp.shape
    n = ys.shape[0]
    groups = len(DIL_PATTERNS)
    width = w_qkv.shape[1] // (3 * groups)
    head_dim = width // DIL_HEADS
    w16 = w_qkv.astype(BF16)
    wo16 = w_o.astype(BF16)
    ones = jnp.ones((width,), F32)
    gains = jnp.stack([row for g in range(groups)
                       for row in (_tiled_gain(q_gain[g], DIL_HEADS), _tiled_gain(k_gain[g], DIL_HEADS), ones)])
    gains = gains.reshape(3 * groups, 1, width)
    tail = min(max(w for w, _ in DIL_PATTERNS), s)
    *qkv, kv32 = _dil_proj(yp, g_mix, w16, gains, tail, [dil for _, dil in DIL_PATTERNS])
    state = None
    for g in range(groups):
        state = _dil_attn(qkv[g], g, state)
    yp_new = _out_proj_heads(state[0], wo16, yp)
    rows_p = [kv32[:, tail - min(w, s):, g * 2 * width:(g + 1) * 2 * width].reshape(b, min(w, s), 2, DIL_HEADS, head_dim)
              for g, (w, _) in enumerate(DIL_PATTERNS)]
    *qkv_s, kvs32 = _dil_proj(ys[None], g_mix, w16, gains, n, [1] * groups)
    q_s = jnp.stack([a[0, 0, :, :width] for a in qkv_s], axis=1).astype(F32).reshape(n, groups, DIL_HEADS, head_dim)
    att = _dil_decode(q_s, kvs32.reshape(n, 2 * groups, DIL_HEADS, head_dim), caches)
    ys_new = _out_proj(att.reshape(n, width), wo16, ys)
    kvs = kvs32.reshape(n, 1, groups, 2, DIL_HEADS, head_dim)
    rows_s = [kvs[:, :, g] for g in range(groups)]
    return yp_new, ys_new, rows_p, rows_s


def _top_blocks(gate, block_idx, axis):
    n = gate.shape[axis]
    chosen = jnp.zeros(gate.shape, jnp.bool_)
    for _ in range(MOBA_TOPK):
        mx = jnp.max(gate, axis=axis, keepdims=True)
        idx = jnp.min(jnp.where(gate == mx, block_idx, n), axis=axis, keepdims=True)
        hit = (block_idx == idx) & (mx > NEG_INF)
        chosen = chosen | hit
        gate = jnp.where(block_idx == idx, NEG_INF, gate)
    return chosen


def _moba_attn_kernel(ti_ref, tj_ref, q_ref, k_ref, vt_ref, km_ref, wo_ref, x_ref, y_ref,
                      qm_ref, bits_ref, m_ref, l_ref, acc_ref, *, slopes, head_dim):
    step = pl.program_id(1)
    i = ti_ref[step]
    j = tj_ref[step]
    blk = MOBA_BLOCK
    lanes = 2 * head_dim
    heads = len(slopes)
    scale = head_dim ** -0.5
    lane = lax.broadcasted_iota(jnp.int32, (1, lanes), 1)
    rk = lax.broadcasted_iota(jnp.int32, (blk, blk), 0)
    rq = lax.broadcasted_iota(jnp.int32, (blk, blk), 1)

    @pl.when(j == i)
    def _():
        nb = km_ref.shape[1]
        bidx = lax.broadcasted_iota(jnp.int32, (nb, blk), 0)
        for pair in range(heads // 2):
            cols = slice(pair * lanes, (pair + 1) * lanes)
            qp = q_ref[0, :, cols]
            kmp = km_ref[0, :, cols]
            for half in range(2):
                h = 2 * pair + half
                own = (lane < head_dim) if half == 0 else (lane >= head_dim)
                base = head_dim if half == 0 else 0
                ones = jnp.where((lane >= base) & (lane < base + ALIBI_PIECES), 1.0, 0.0)
                qm_ref[h] = jnp.where(own, qp * (scale * LOG2E), ones).astype(BF16)
                gate = _dot_nt(kmp.astype(BF16), jnp.where(own, qp, 0.0).astype(BF16))
                chosen = _top_blocks(jnp.where(bidx < i, gate, NEG_INF), bidx, 0)
                bits_ref[h] = jnp.sum(jnp.where(chosen, jnp.left_shift(1, bidx), 0), axis=0, keepdims=True)

    def softmax_step(h, s, first):
        if first:
            s = jnp.where(rk <= rq, s, NEG_INF)
            m_new = jnp.max(s, axis=0, keepdims=True)
            p = jnp.exp2(s - m_new)
            l_new = jnp.sum(p, axis=0, keepdims=True)
            alpha = None
        else:
            offset = (slopes[h] * LOG2E) * ((j - i) * blk).astype(F32)
            picked = (jnp.right_shift(bits_ref[h], j) & 1) == 1
            m_prev = m_ref[h]
            m_new = jnp.where(picked, jnp.maximum(m_prev, jnp.max(s, axis=0, keepdims=True) + offset), m_prev)
            alpha = jnp.exp2(m_prev - m_new)
            p = jnp.exp2(s - jnp.where(picked, m_new - offset, MASKED_MAX))
            l_new = alpha * l_ref[h] + jnp.sum(p, axis=0, keepdims=True)
        m_ref[h] = m_new
        l_ref[h] = l_new
        return p.astype(BF16), alpha

    def attend(first):
        for h0 in range(0, heads, HEAD_GROUP):
            group = range(h0, h0 + HEAD_GROUP)
            scores = [_dot_nt(k_ref[0, :, h * lanes:(h + 1) * lanes], qm_ref[h]) for h in group]
            probs = [softmax_step(h, s, first) for h, s in zip(group, scores)]
            for h, (p, alpha) in zip(group, probs):
                rows = slice(h * head_dim, (h + 1) * head_dim)
                pv = _dot(vt_ref[0, rows, :], p)
                if first:
                    acc_ref[rows, :] = pv
                else:
                    acc_ref[rows, :] = alpha * acc_ref[rows, :] + pv

    @pl.when(j == i)
    def _():
        attend(True)

    @pl.when(j != i)
    def _():
        attend(False)

    @pl.when((j == i - 1) | (i == 0))
    def _():
        parts = [acc_ref[h * head_dim:(h + 1) * head_dim, :] / l_ref[h] for h in range(heads)]
        out_t = jnp.concatenate(parts, axis=0).astype(BF16)
        y_ref[0] = x_ref[0] + _dot_tn(out_t, wo_ref[...])


def _moba_attn(q32, k16, vt16, km, wo16, x):
    b, s, width = q32.shape
    d = x.shape[-1]
    head_dim = width // MOBA_HEADS
    nt = s // MOBA_BLOCK
    assert nt <= 32, "the chosen (strictly earlier) blocks are kept as bits 0..30 of an int32"
    ti = [i for i in range(nt) for _ in range(i + 1)]
    tj = [j for i in range(nt) for j in [i, *range(i)]]
    q_tile = lambda n, t, ti, tj: (n, ti[t], 0)
    grid_spec = pltpu.PrefetchScalarGridSpec(
        num_scalar_prefetch=2,
        grid=(b, len(ti)),
        in_specs=[pl.BlockSpec((1, MOBA_BLOCK, width), q_tile),
                  pl.BlockSpec((1, MOBA_BLOCK, 2 * width), lambda n, t, ti, tj: (n, tj[t], 0)),
                  pl.BlockSpec((1, width, MOBA_BLOCK), lambda n, t, ti, tj: (n, 0, tj[t])),
                  pl.BlockSpec((1, nt, width), lambda n, t, ti, tj: (n, 0, 0)),
                  pl.BlockSpec(wo16.shape, lambda n, t, ti, tj: (0, 0), pipeline_mode=pl.Buffered(1)),
                  pl.BlockSpec((1, MOBA_BLOCK, d), q_tile)],
        out_specs=pl.BlockSpec((1, MOBA_BLOCK, d), q_tile),
        scratch_shapes=[pltpu.VMEM((MOBA_HEADS, MOBA_BLOCK, 2 * head_dim), BF16),
                        pltpu.VMEM((MOBA_HEADS, 1, MOBA_BLOCK), jnp.int32),
                        pltpu.VMEM((MOBA_HEADS, 1, MOBA_BLOCK), F32),
                        pltpu.VMEM((MOBA_HEADS, 1, MOBA_BLOCK), F32),
                        pltpu.VMEM((width, MOBA_BLOCK), F32)])
    return pl.pallas_call(
        functools.partial(_moba_attn_kernel, slopes=_alibi_slopes(MOBA_HEADS), head_dim=head_dim),
        grid_spec=grid_spec,
        out_shape=jax.ShapeDtypeStruct((b, s, d), F32),
        compiler_params=_params("arbitrary", "arbitrary"),
        name="moba_attn",
    )(jnp.asarray(ti, jnp.int32), jnp.asarray(tj, jnp.int32), q32, k16, vt16, km, wo16, x)


def _moba_decode_kernel(pt_ref, q_ref, kvn_ref, slope_ref, *refs, past_len):
    page_refs = refs[:2 * DECODE_BLOCKS]
    o_ref, gate_ref, m_ref, l_ref, acc_ref = refs[2 * DECODE_BLOCKS:]
    j = pl.program_id(1)
    heads, head_dim, keys = q_ref.shape[1:]
    scale = head_dim ** -0.5
    q = q_ref[0]
    slope = slope_ref[...]
    lane = lax.broadcasted_iota(jnp.int32, (1, 1, keys), 2)

    @pl.when(j == 0)
    def _():
        gate_ref[...] = jnp.full(gate_ref.shape, NEG_INF, F32)
        m_ref[...] = jnp.zeros(m_ref.shape, F32)
        l_ref[...] = jnp.zeros(l_ref.shape, F32)
        acc_ref[...] = jnp.zeros(acc_ref.shape, F32)

    for sub in range(DECODE_BLOCKS):
        block = j * DECODE_BLOCKS + sub
        p0_ref, p1_ref = page_refs[2 * sub:2 * sub + 2]
        scores = []
        for half, page_ref in enumerate((p0_ref, p1_ref)):
            dist = (past_len - block * MOBA_BLOCK - half * keys - lane).astype(F32)
            qk = jnp.sum(q * page_ref[0, 0], axis=1, keepdims=True)
            scores.append(qk * scale - slope * dist)
        m = jnp.maximum(jnp.max(scores[0], axis=-1, keepdims=True), jnp.max(scores[1], axis=-1, keepdims=True))
        p0 = jnp.exp(scores[0] - m)
        p1 = jnp.exp(scores[1] - m)
        here = lane == block
        m_ref[...] = jnp.where(here, m, m_ref[...])
        l_ref[...] = jnp.where(here, jnp.sum(p0 + p1, axis=-1, keepdims=True), l_ref[...])
        acc = jnp.sum(p0 * p0_ref[0, 1] + p1 * p1_ref[0, 1], axis=-1, keepdims=True)
        acc_ref[...] = jnp.where(here, acc, acc_ref[...])
        k_mean = jnp.sum(p0_ref[0, 0] + p1_ref[0, 0], axis=-1, keepdims=True) * (1.0 / MOBA_BLOCK)
        gate = jnp.sum(q[:, :, 0:1] * _as_dot_operand(k_mean), axis=1, keepdims=True)
        gate_ref[...] = jnp.where(here, gate, gate_ref[...])

    @pl.when(j == pl.num_programs(1) - 1)
    def _():
        chosen = _top_blocks(gate_ref[...], jnp.broadcast_to(lane, gate_ref.shape), 2)
        kn = _as_dot_operand(kvn_ref[0, 0])
        vn = _as_dot_operand(kvn_ref[0, 1])
        s_new = jnp.sum(q * kn, axis=1, keepdims=True)[:, :, 0:1] * scale
        m_all = m_ref[...]
        m_tot = jnp.maximum(jnp.max(jnp.where(chosen, m_all, NEG_INF), axis=-1, keepdims=True), s_new)
        w = jnp.where(chosen, jnp.exp(m_all - m_tot), 0.0)
        e_new = jnp.exp(s_new - m_tot)
        den = jnp.sum(w * l_ref[...], axis=-1, keepdims=True) + e_new
        o_ref[0] = (jnp.sum(w * acc_ref[...], axis=-1, keepdims=True) + _as_dot_operand(e_new) * vn) / den


def _moba_decode(q, kv_new, cache, page_table):
    n, heads, head_dim = q.shape
    n_pages = page_table.shape[1]
    per_blk = MOBA_BLOCK // PAGE_SIZE
    assert cache.shape[1] == PAGE_SIZE and per_blk == 2 and n_pages % (per_blk * DECODE_BLOCKS) == 0
    nb = n_pages // per_blk
    assert nb <= PAGE_SIZE, "per-block statistics are kept one block per lane"
    pages_t = jnp.transpose(cache, (0, 2, 3, 4, 1))
    q_b = jnp.broadcast_to(_as_dot_operand(q)[..., None], (n, heads, head_dim, PAGE_SIZE))
    slopes = jnp.broadcast_to(jnp.asarray(_alibi_slopes(heads), F32)[:, None, None], (heads, 1, PAGE_SIZE))
    row4 = lambda i, j, pt: (i, 0, 0, 0)
    row5 = lambda i, j, pt: (i, 0, 0, 0, 0)
    pages_per_step = per_blk * DECODE_BLOCKS

    def page_spec(k):
        return pl.BlockSpec((1, 2, heads, head_dim, PAGE_SIZE),
                            lambda i, j, pt: (pt[i, pages_per_step * j + k], 0, 0, 0, 0))

    grid_spec = pltpu.PrefetchScalarGridSpec(
        num_scalar_prefetch=1,
        grid=(n, n_pages // pages_per_step),
        in_specs=[pl.BlockSpec((1, heads, head_dim, PAGE_SIZE), row4),
                  pl.BlockSpec((1, 2, heads, head_dim, 1), row5),
                  pl.BlockSpec(slopes.shape, lambda i, j, pt: (0, 0, 0))]
        + [page_spec(k) for k in range(pages_per_step)],
        out_specs=pl.BlockSpec((1, heads, head_dim, 1), row4),
        scratch_shapes=[pltpu.VMEM((heads, 1, PAGE_SIZE), F32),
                        pltpu.VMEM((heads, 1, PAGE_SIZE), F32),
                        pltpu.VMEM((heads, 1, PAGE_SIZE), F32),
                        pltpu.VMEM((heads, head_dim, PAGE_SIZE), F32)])
    o = pl.pallas_call(
        functools.partial(_moba_decode_kernel, past_len=n_pages * PAGE_SIZE),
        grid_spec=grid_spec,
        out_shape=jax.ShapeDtypeStruct((n, heads, head_dim, 1), F32),
        compiler_params=_params("arbitrary", "arbitrary"),
        name="moba_decode",
    )(page_table, q_b, kv_new[..., None], slopes, *([pages_t] * pages_per_step))
    return o.reshape(n, heads, head_dim)


def _moba_layer(yp, ys, cache, page_table, g_mix, w_qkv, q_gain, k_gain, w_o):
    b, s, d = yp.shape
    n = ys.shape[0]
    width = w_qkv.shape[1] // 3
    head_dim = width // MOBA_HEADS
    w16 = w_qkv.astype(BF16)
    wo16 = w_o.astype(BF16)
    gains = jnp.stack([_tiled_gain(q_gain, MOBA_HEADS), _tiled_gain(k_gain, MOBA_HEADS), jnp.ones((width,), F32)])
    gains = gains.reshape(3, 1, width)
    q32, kv32, k16, vt16, km = _moba_proj(yp, g_mix, w16, gains, prompt=True)
    yp_new = _moba_attn(q32, k16, vt16, km.reshape(b, s // MOBA_BLOCK, width), wo16, yp)
    rows_p = kv32.reshape(b, s, 2, MOBA_HEADS, head_dim)
    qs32, kvs32 = _moba_proj(ys[None], g_mix, w16, gains, prompt=False)
    att_s = _moba_decode(qs32.reshape(n, MOBA_HEADS, head_dim), kvs32.reshape(n, 2, MOBA_HEADS, head_dim),
                         cache, page_table)
    ys_new = _out_proj(att_s.reshape(n, width), wo16, ys)
    rows_s = kvs32.reshape(n, 1, 2, MOBA_HEADS, head_dim)
    return yp_new, ys_new, rows_p, rows_s


def kernel(x_prompt, x_sample, state_pool, cache_dil0, cache_dil1, cache_dil2, state_conv, cache_moba, page_table,
           norm_mix, norm_ffn, pool_w, pool_scale, dil_w_qkv, dil_q_gain, dil_k_gain, dil_w_o,
           moba_w_qkv, moba_q_gain, moba_k_gain, moba_w_o, ffn_w_in, ffn_conv_w, ffn_conv_b, ffn_w_out):
    assert x_sample.shape[1] == 1, "the sample group decodes one token per sequence"
    depth = norm_mix.shape[0]
    yp, ys = x_prompt, x_sample[:, 0]
    pool_p, pool_s, moba_p, moba_s, conv_p, conv_s = [], [], [], [], [], []
    dil_p = [[] for _ in DIL_PATTERNS]
    dil_s = [[] for _ in DIL_PATTERNS]
    for layer in range(depth):
        kind, j = layer % N_MIXERS, layer // N_MIXERS
        g_mix = norm_mix[layer][None]
        if kind == 0:
            w16 = pool_w[j].astype(BF16)
            scale = pool_scale[j][None]
            yp, tail = _pool_prompt(yp, g_mix, w16, scale)
            ys, hs = _pool_sample(ys, state_pool[j].transpose(1, 0, 2), g_mix, w16, scale)
            pool_p.append(tail[:, POOL_CARRY - POOL_HIST:])
            pool_s.append(jnp.concatenate([state_pool[j][:, 1:], hs[:, None]], axis=1))
        elif kind == 1:
            caches = [c[j] for c in (cache_dil0, cache_dil1, cache_dil2)]
            yp, ys, rows_p, rows_s = _dil_layer(yp, ys, caches, g_mix, dil_w_qkv[j], dil_q_gain[j], dil_k_gain[j],
                                                dil_w_o[j])
            for g in range(len(DIL_PATTERNS)):
                dil_p[g].append(rows_p[g])
                dil_s[g].append(rows_s[g])
        else:
            yp, ys, rows_p, rows_s = _moba_layer(yp, ys, cache_moba[j], page_table, g_mix, moba_w_qkv[j],
                                                 moba_q_gain[j], moba_k_gain[j], moba_w_o[j])
            moba_p.append(rows_p)
            moba_s.append(rows_s)
        g_ffn = norm_ffn[layer][None]
        win16 = ffn_w_in[layer].astype(BF16)
        wout16 = ffn_w_out[layer].astype(BF16)
        conv_w, conv_b = ffn_conv_w[layer], ffn_conv_b[layer][None]
        yp, tail = _ffn_prompt(yp, g_ffn, win16, conv_w, conv_b, wout16)
        ys, ua, ug = _ffn_sample(ys, g_ffn, win16, state_conv[layer].transpose(1, 0, 2), conv_w, conv_b, wout16)
        conv_p.append(tail[:, CONV_CARRY - CONV_HIST:])
        u_new = jnp.concatenate([ua, ug], axis=-1)[:, None]
        conv_s.append(jnp.concatenate([state_conv[layer][:, 1:], u_new], axis=1))
    stack = lambda xs: jnp.stack(xs, axis=0)
    return (yp, ys[:, None], stack(pool_p), stack(pool_s),
            stack(dil_p[0]), stack(dil_p[1]), stack(dil_p[2]),
            stack(dil_s[0]), stack(dil_s[1]), stack(dil_s[2]),
            stack(moba_p), stack(moba_s), stack(conv_p), stack(conv_s))
```

```python
import functools

import jax
import jax.numpy as jnp
from jax import lax
from jax.experimental import pallas as pl
from jax.experimental.pallas import tpu as pltpu

F32 = jnp.float32
BF16 = jnp.bfloat16

N_MIXERS = 3
POOL_WINDOWS = (2, 4, 8, 16)
POOL_HIST = max(POOL_WINDOWS) - 1
POOL_CARRY = 16
DIL_PATTERNS = ((128, 1), (512, 4), (2048, 16))
DIL_BAND = 128
DIL_HEADS = 8
MOBA_BLOCK = 256
MOBA_TOPK = 3
MOBA_HEADS = 16
HEAD_GROUP = 16
PAST_PER_STEP = 2
DECODE_BLOCKS = 2
PAGE_SIZE = 128
CONV_WIDTH = 3
CONV_HIST = CONV_WIDTH - 1
CONV_CARRY = 8
RMS_EPS = 1e-6
LOG2E = 1.4426950408889634
ALIBI_PIECES = 3
NEG_INF = float("-inf")
MASKED_MAX = 1e30
VMEM_LIMIT = 56 * 1024 * 1024


def _params(*sem):
    return pltpu.CompilerParams(dimension_semantics=sem, vmem_limit_bytes=VMEM_LIMIT)


def _resident(shape):
    zeros = (0,) * len(shape)
    return pl.BlockSpec(shape, lambda *_: zeros, pipeline_mode=pl.Buffered(1))


def _rms(x, g):
    return x * lax.rsqrt(jnp.mean(x * x, axis=-1, keepdims=True) + RMS_EPS) * g


def _dot(a, b):
    return jnp.dot(a, b, preferred_element_type=F32)


def _dot_nt(a, b, precision=None):
    return lax.dot_general(a, b, (((1,), (1,)), ((), ())), preferred_element_type=F32, precision=precision)


def _as_dot_operand(x):
    return x.astype(BF16).astype(F32)


def _dot_tn(a, b):
    return lax.dot_general(a, b, (((0,), (0,)), ((), ())), preferred_element_type=F32)


def _alibi_slopes(n):
    return [2.0 ** (-8.0 * i / n) for i in range(1, n + 1)]


def _pool_prompt_kernel(x_ref, g_ref, w_ref, sc_ref, y_ref, tail_ref, ext_ref, *, tile):
    t = pl.program_id(1)
    d_model = x_ref.shape[-1]
    gdim = d_model // len(POOL_WINDOWS)

    @pl.when(t == 0)
    def _():
        ext_ref[0:POOL_CARRY, :] = jnp.zeros((POOL_CARRY, d_model), F32)

    x = x_ref[0]
    ext_ref[POOL_CARRY:POOL_CARRY + tile, :] = _rms(x, g_ref[...])
    row = t * tile + lax.broadcasted_iota(jnp.int32, (tile, 1), 0)
    for gi, win in enumerate(POOL_WINDOWS):
        cols = slice(gi * gdim, (gi + 1) * gdim)
        cur = ext_ref[POOL_CARRY:POOL_CARRY + tile, cols]
        acc = cur
        for j in range(1, win):
            acc = acc + ext_ref[POOL_CARRY - j:POOL_CARRY - j + tile, cols]
        cnt = jnp.minimum(row + 1, win).astype(F32)
        d = acc / cnt - cur
        y = _dot(d.astype(BF16), w_ref[gi])
        y_ref[0, :, cols] = x[:, cols] + y * sc_ref[:, cols]
    last = ext_ref[tile:tile + POOL_CARRY, :]
    ext_ref[0:POOL_CARRY, :] = last

    @pl.when(t == pl.num_programs(1) - 1)
    def _():
        tail_ref[0] = last


def _pool_prompt(x, g, w16, sc, tile=512):
    b, s, d = x.shape
    tile = min(tile, s)
    return pl.pallas_call(
        functools.partial(_pool_prompt_kernel, tile=tile),
        grid=(b, s // tile),
        in_specs=[pl.BlockSpec((1, tile, d), lambda i, t: (i, t, 0)),
                  _resident((1, d)), _resident(w16.shape), _resident((1, d))],
        out_specs=[pl.BlockSpec((1, tile, d), lambda i, t: (i, t, 0)),
                   pl.BlockSpec((1, POOL_CARRY, d), lambda i, t: (i, 0, 0))],
        out_shape=[jax.ShapeDtypeStruct((b, s, d), F32), jax.ShapeDtypeStruct((b, POOL_CARRY, d), F32)],
        scratch_shapes=[pltpu.VMEM((POOL_CARRY + tile, d), F32)],
        compiler_params=_params("arbitrary", "arbitrary"),
        name="pool_prompt",
    )(x, g, w16, sc)


def _pool_sample_kernel(x_ref, st_ref, g_ref, w_ref, sc_ref, y_ref, h_ref):
    d_model = x_ref.shape[-1]
    gdim = d_model // len(POOL_WINDOWS)
    x = x_ref[...]
    h = _rms(x, g_ref[...])
    h_ref[...] = h
    for gi, win in enumerate(POOL_WINDOWS):
        cols = slice(gi * gdim, (gi + 1) * gdim)
        cur = h[:, cols]
        acc = cur
        for j in range(1, win):
            acc = acc + st_ref[POOL_HIST - j, :, cols]
        d = acc / float(win) - cur
        y = _dot(d.astype(BF16), w_ref[gi])
        y_ref[:, cols] = x[:, cols] + y * sc_ref[:, cols]


def _pool_sample(x, state_t, g, w16, sc):
    n, d = x.shape
    return pl.pallas_call(
        _pool_sample_kernel,
        out_shape=[jax.ShapeDtypeStruct((n, d), F32), jax.ShapeDtypeStruct((n, d), F32)],
        compiler_params=pltpu.CompilerParams(vmem_limit_bytes=VMEM_LIMIT),
        name="pool_sample",
    )(x, state_t, g, w16, sc)


def _silu(g):
    return g * (1.0 / (1.0 + jnp.exp(-g)))


def _ffn_prompt_kernel(y_ref, g_ref, win_ref, cw_ref, cb_ref, wout_ref, o_ref, tail_ref,
                       u_even_ref, u_odd_ref, carry_ref, acc_ref, *, tile, chunk):
    t = pl.program_id(1)
    hidden = wout_ref.shape[0]
    u_bufs = (u_even_ref, u_odd_ref)

    @pl.when(t == 0)
    def _():
        carry_ref[...] = jnp.zeros(carry_ref.shape, F32)

    y = y_ref[0]
    h = _rms(y, g_ref[...]).astype(BF16)
    n_chunks = hidden // chunk

    def up_project(c):
        us = []
        for part in range(2):
            cols = slice(part * hidden + c * chunk, part * hidden + (c + 1) * chunk)
            u = _dot(h, win_ref[:, cols])
            u_bufs[c % 2][part, 0:CONV_CARRY, :] = carry_ref[:, cols]
            u_bufs[c % 2][part, CONV_CARRY:CONV_CARRY + tile, :] = u
            carry_ref[:, cols] = u[tile - CONV_CARRY:tile, :]
            us.append(u)
        return us

    def conv_gate(c, us):
        conv = []
        for part in range(2):
            cols = slice(part * hidden + c * chunk, part * hidden + (c + 1) * chunk)
            cv = cb_ref[:, cols] + cw_ref[2:3, cols] * us[part]
            cv = cv + cw_ref[1:2, cols] * u_bufs[c % 2][part, CONV_CARRY - 1:CONV_CARRY - 1 + tile, :]
            cv = cv + cw_ref[0:1, cols] * u_bufs[c % 2][part, CONV_CARRY - 2:CONV_CARRY - 2 + tile, :]
            conv.append(cv)
        return (_silu(conv[1]) * conv[0]).astype(BF16)

    us_next = up_project(0)
    for c in range(n_chunks):
        us = us_next
        if c + 1 < n_chunks:
            us_next = up_project(c + 1)
        contrib = _dot(conv_gate(c, us), wout_ref[c * chunk:(c + 1) * chunk, :])
        if c == 0:
            acc_ref[...] = contrib
        else:
            acc_ref[...] += contrib
    o_ref[0] = y + acc_ref[...]

    @pl.when(t == pl.num_programs(1) - 1)
    def _():
        tail_ref[0] = carry_ref[...]


def _ffn_prompt(y, g, win16, cw, cb, wout16, tile=512, chunk=256):
    b, s, d = y.shape
    hidden = wout16.shape[0]
    tile = min(tile, s)
    return pl.pallas_call(
        functools.partial(_ffn_prompt_kernel, tile=tile, chunk=chunk),
        grid=(b, s // tile),
        in_specs=[pl.BlockSpec((1, tile, d), lambda i, t: (i, t, 0)),
                  _resident((1, d)), _resident(win16.shape), _resident(cw.shape), _resident(cb.shape),
                  _resident(wout16.shape)],
        out_specs=[pl.BlockSpec((1, tile, d), lambda i, t: (i, t, 0)),
                   pl.BlockSpec((1, CONV_CARRY, 2 * hidden), lambda i, t: (i, 0, 0))],
        out_shape=[jax.ShapeDtypeStruct((b, s, d), F32),
                   jax.ShapeDtypeStruct((b, CONV_CARRY, 2 * hidden), F32)],
        scratch_shapes=[pltpu.VMEM((2, CONV_CARRY + tile, chunk), F32),
                        pltpu.VMEM((2, CONV_CARRY + tile, chunk), F32),
                        pltpu.VMEM((CONV_CARRY, 2 * hidden), F32),
                        pltpu.VMEM((tile, d), F32)],
        compiler_params=_params("arbitrary", "arbitrary"),
        name="ffn_prompt",
    )(y, g, win16, cw, cb, wout16)


def _ffn_sample_kernel(y_ref, g_ref, wa_ref, wg_ref, ha_ref, hg_ref, cwa_ref, cwg_ref, cba_ref, cbg_ref,
                       wout_ref, o_ref, ua_ref, ug_ref):
    c = pl.program_id(0)
    y = y_ref[...]
    h = _rms(y, g_ref[...]).astype(BF16)

    def conv(w_ref, hist_ref, cw_ref, cb_ref, u_out_ref):
        u = _dot(h, w_ref[...])
        u_out_ref[...] = u
        return cb_ref[...] + cw_ref[0:1, :] * hist_ref[0] + cw_ref[1:2, :] * hist_ref[1] + cw_ref[2:3, :] * u

    ca = conv(wa_ref, ha_ref, cwa_ref, cba_ref, ua_ref)
    cg = conv(wg_ref, hg_ref, cwg_ref, cbg_ref, ug_ref)
    contrib = _dot((_silu(cg) * ca).astype(BF16), wout_ref[...])

    @pl.when(c == 0)
    def _():
        o_ref[...] = y + contrib

    @pl.when(c != 0)
    def _():
        o_ref[...] += contrib


def _ffn_sample(y, g, win16, hist_t, cw, cb, wout16, chunk=256):
    n, d = y.shape
    hidden = wout16.shape[0]
    nc = hidden // chunk
    a_col = lambda c: (0, c)
    g_col = lambda c: (0, c + nc)
    return pl.pallas_call(
        _ffn_sample_kernel,
        grid=(nc,),
        in_specs=[_resident((n, d)), _resident((1, d)),
                  pl.BlockSpec((d, chunk), a_col), pl.BlockSpec((d, chunk), g_col),
                  pl.BlockSpec((CONV_HIST, n, chunk), lambda c: (0, 0, c)),
                  pl.BlockSpec((CONV_HIST, n, chunk), lambda c: (0, 0, c + nc)),
                  pl.BlockSpec((CONV_WIDTH, chunk), a_col), pl.BlockSpec((CONV_WIDTH, chunk), g_col),
                  pl.BlockSpec((1, chunk), a_col), pl.BlockSpec((1, chunk), g_col),
                  pl.BlockSpec((chunk, d), lambda c: (c, 0))],
        out_specs=[pl.BlockSpec((n, d), lambda c: (0, 0)),
                   pl.BlockSpec((n, chunk), a_col), pl.BlockSpec((n, chunk), a_col)],
        out_shape=[jax.ShapeDtypeStruct((n, d), F32), jax.ShapeDtypeStruct((n, hidden), F32),
                   jax.ShapeDtypeStruct((n, hidden), F32)],
        compiler_params=_params("arbitrary"),
        name="ffn_sample",
    )(y, g, win16, win16, hist_t, hist_t, cw, cw, cb, cb, wout16)


def _out_proj_kernel(a_ref, w_ref, x_ref, y_ref):
    y_ref[...] = x_ref[...] + _dot(a_ref[...].astype(BF16), w_ref[...])


def _out_proj(a, w16, x, tile=512):
    n, d = x.shape
    tile = min(tile, n)
    return pl.pallas_call(
        _out_proj_kernel,
        grid=(n // tile,),
        in_specs=[pl.BlockSpec((tile, a.shape[1]), lambda t: (t, 0)), _resident(w16.shape),
                  pl.BlockSpec((tile, d), lambda t: (t, 0))],
        out_specs=pl.BlockSpec((tile, d), lambda t: (t, 0)),
        out_shape=jax.ShapeDtypeStruct((n, d), F32),
        compiler_params=_params("arbitrary"),
        name="out_proj",
    )(a, w16, x)


def _out_proj_heads_kernel(a_ref, w_ref, x_ref, y_ref):
    a = jnp.concatenate([a_ref[0, h] for h in range(a_ref.shape[1])], axis=1)
    y_ref[0] = x_ref[0] + _dot(a.astype(BF16), w_ref[...])


def _out_proj_heads(a, w16, x, tile=512):
    b, heads, s, head_dim = a.shape
    d = x.shape[-1]
    tile = min(tile, s)
    return pl.pallas_call(
        _out_proj_heads_kernel,
        grid=(b, s // tile),
        in_specs=[pl.BlockSpec((1, heads, tile, head_dim), lambda i, t: (i, 0, t, 0)), _resident(w16.shape),
                  pl.BlockSpec((1, tile, d), lambda i, t: (i, t, 0))],
        out_specs=pl.BlockSpec((1, tile, d), lambda i, t: (i, t, 0)),
        out_shape=jax.ShapeDtypeStruct((b, s, d), F32),
        compiler_params=_params("arbitrary", "arbitrary"),
        name="out_proj_heads",
    )(a, w16, x)


def _head_norm(r, gain, head_dim):
    lanes = 128
    low = lax.broadcasted_iota(jnp.int32, (1, lanes), 1) < head_dim
    cols = []
    for v in range(r.shape[1] // lanes):
        seg = r[:, v * lanes:(v + 1) * lanes]
        sq = seg * seg
        if head_dim == lanes:
            ms = jnp.mean(sq, axis=-1, keepdims=True)
        else:
            lo = jnp.sum(jnp.where(low, sq, 0.0), axis=-1, keepdims=True)
            hi = jnp.sum(jnp.where(low, 0.0, sq), axis=-1, keepdims=True)
            ms = jnp.where(low, lo, hi) * (1.0 / head_dim)
        cols.append(seg * lax.rsqrt(ms + RMS_EPS))
    return jnp.concatenate(cols, axis=1) * gain


def _tiled_gain(gain, heads):
    return jnp.tile(gain.astype(F32), heads)


def _dil_proj_kernel(x_ref, g_ref, w_ref, gain_ref, o0_ref, o1_ref, o2_ref, kv32_ref, h_ref, r_ref,
                     *, dils, head_dim):
    j = pl.program_id(2)
    g = j // 3
    c = j % 3
    chunks, tile, lanes = r_ref.shape

    @pl.when(j == 0)
    def _():
        h_ref[...] = _rms(x_ref[0], g_ref[...]).astype(BF16)

    r = _dot(h_ref[...], w_ref[...])

    def emit(val):
        @pl.when(c > 0)
        def _():
            kv32_ref[0] = val

        for gi, (o_ref, dil) in enumerate(zip((o0_ref, o1_ref, o2_ref), dils)):
            @pl.when(g == gi)
            def _(o_ref=o_ref, dil=dil):
                if dil == 1:
                    o_ref[0, 0] = val.astype(BF16)
                    return
                for ch in range(chunks):
                    r_ref[ch] = val[:, ch * lanes:(ch + 1) * lanes]
                for res in range(dil):
                    for ch in range(chunks):
                        o_ref[0, res, :, ch * lanes:(ch + 1) * lanes] = (
                            r_ref[ch, pl.ds(res, tile // dil, stride=dil), :].astype(BF16))

    @pl.when(c < 2)
    def _():
        emit(_head_norm(r, gain_ref[0], head_dim))

    @pl.when(c == 2)
    def _():
        emit(r)


def _dil_proj(x, g, w16, gains, tail_rows, dils, tile=1024):
    b, s, d = x.shape
    groups = len(dils)
    width = w16.shape[1] // (3 * groups)
    tile = min(tile, s, tail_rows)
    t0 = (s - tail_rows) // tile

    def kv_map(i, t, j):
        col = (j // 3) * 2 + jnp.maximum(j % 3 - 1, 0)
        return i, jnp.maximum(t - t0, 0), jnp.where(t >= t0, col, 0)

    def group_spec(gi, dil):
        return pl.BlockSpec((1, dil, tile // dil, width),
                            lambda i, t, j: (i, 0, t, jnp.clip(j - 3 * gi, 0, 2)))

    return pl.pallas_call(
        functools.partial(_dil_proj_kernel, dils=tuple(dils), head_dim=width // DIL_HEADS),
        grid=(b, s // tile, 3 * groups),
        in_specs=[pl.BlockSpec((1, tile, d), lambda i, t, j: (i, t, 0)), _resident((1, d)),
                  pl.BlockSpec((d, width), lambda i, t, j: (0, j)),
                  pl.BlockSpec((1, 1, width), lambda i, t, j: (j, 0, 0))],
        out_specs=[group_spec(gi, dil) for gi, dil in enumerate(dils)]
        + [pl.BlockSpec((1, tile, width), kv_map)],
        out_shape=[jax.ShapeDtypeStruct((b, dil, s // dil, 3 * width), BF16) for dil in dils]
        + [jax.ShapeDtypeStruct((b, tail_rows, groups * 2 * width), F32)],
        scratch_shapes=[pltpu.VMEM((tile, d), BF16), pltpu.VMEM((width // 128, tile, 128), F32)],
        compiler_params=_params("arbitrary", "arbitrary", "arbitrary"),
        name="dil_proj",
    )(x, g, w16, gains)


def _moba_proj_kernel(x_ref, g_ref, w_ref, gain_ref, *refs, head_dim, prompt):
    if prompt:
        wvt_ref, q32_ref, kv32_ref, k16_ref, vt16_ref, km_ref, h_ref = refs
    else:
        q32_ref, kv32_ref, h_ref = refs
    j = pl.program_id(2)

    @pl.when(j == 0)
    def _():
        h_ref[...] = _rms(x_ref[0], g_ref[...]).astype(BF16)

    r = _dot(h_ref[...], w_ref[...])

    @pl.when(j == 0)
    def _():
        q32_ref[0] = _head_norm(r, gain_ref[0], head_dim)

    @pl.when(j == 1)
    def _():
        rn = _head_norm(r, gain_ref[0], head_dim)
        kv32_ref[0] = rn
        if prompt:
            lanes = 2 * head_dim
            lane = lax.broadcasted_iota(jnp.int32, (1, lanes), 1)
            in_block = (lax.broadcasted_iota(jnp.int32, (rn.shape[0], 1), 0) % MOBA_BLOCK).astype(F32)
            for h, slope in enumerate(_alibi_slopes(rn.shape[1] // head_dim)):
                kp = rn[:, (h // 2) * lanes:(h // 2 + 1) * lanes]
                own = (lane < head_dim) if h % 2 == 0 else (lane >= head_dim)
                base = head_dim if h % 2 == 0 else 0
                rest = in_block * (slope * LOG2E)
                term = jnp.zeros((1, lanes), F32)
                for piece in range(ALIBI_PIECES):
                    part = rest if piece == ALIBI_PIECES - 1 else rest.astype(BF16).astype(F32)
                    term = jnp.where(lane == base + piece, part, term)
                    rest = rest - part
                k16_ref[0, :, h * lanes:(h + 1) * lanes] = jnp.where(own, kp, term).astype(BF16)
            for blk in range(rn.shape[0] // MOBA_BLOCK):
                rows = rn[blk * MOBA_BLOCK:(blk + 1) * MOBA_BLOCK, :]
                km_ref[0, blk] = jnp.mean(rows, axis=0, keepdims=True)

    @pl.when(j == 2)
    def _():
        kv32_ref[0] = r
        if prompt:
            vt16_ref[0] = _dot_nt(wvt_ref[...], h_ref[...]).astype(BF16)


def _moba_proj(x, g, w16, gains, prompt, tile=1024):
    b, s, d = x.shape
    width = w16.shape[1] // 3
    tile = min(tile, s)
    in_specs = [pl.BlockSpec((1, tile, d), lambda i, t, j: (i, t, 0)), _resident((1, d)),
                pl.BlockSpec((d, width), lambda i, t, j: (0, j)),
                pl.BlockSpec((1, 1, width), lambda i, t, j: (j, 0, 0))]
    args = [x, g, w16, gains]
    out_specs = [pl.BlockSpec((1, tile, width), lambda i, t, j: (i, t, 0)),
                 pl.BlockSpec((1, tile, width), lambda i, t, j: (i, t, jnp.maximum(j - 1, 0)))]
    out_shape = [jax.ShapeDtypeStruct((b, s, width), F32), jax.ShapeDtypeStruct((b, s, 2 * width), F32)]
    if prompt:
        in_specs.append(_resident((width, d)))
        args.append(w16[:, 2 * width:].T)
        out_specs += [pl.BlockSpec((1, tile, 2 * width), lambda i, t, j: (i, t, 0)),
                      pl.BlockSpec((1, width, tile), lambda i, t, j: (i, 0, t)),
                      pl.BlockSpec((1, tile // MOBA_BLOCK, 1, width), lambda i, t, j: (i, t, 0, 0))]
        out_shape += [jax.ShapeDtypeStruct((b, s, 2 * width), BF16), jax.ShapeDtypeStruct((b, width, s), BF16),
                      jax.ShapeDtypeStruct((b, s // MOBA_BLOCK, 1, width), F32)]
    return pl.pallas_call(
        functools.partial(_moba_proj_kernel, head_dim=width // MOBA_HEADS, prompt=prompt),
        grid=(b, s // tile, 3),
        in_specs=in_specs,
        out_specs=out_specs,
        out_shape=out_shape,
        scratch_shapes=[pltpu.VMEM((tile, d), BF16)],
        compiler_params=_params("arbitrary", "arbitrary", "arbitrary"),
        name="moba_proj",
    )(*args)


def _dil_attn_kernel(q_ref, kp_ref, ko_ref, vp_ref, vo_ref, *refs, dil, slopes, head_dim, chained, last):
    refs = list(refs)
    oin_ref, lin_ref = (refs.pop(0), refs.pop(0)) if chained else (None, None)
    o_ref = refs.pop(0)
    l_ref = None if last else refs.pop(0)
    n = pl.program_id(1)
    r = pl.program_id(2)
    band = DIL_BAND
    width = len(slopes) * head_dim
    scale = head_dim ** -0.5
    rows = slice(None) if dil == 1 else pl.ds(r, band, stride=dil)
    qi = lax.broadcasted_iota(jnp.int32, (band, band), 0)
    kj = lax.broadcasted_iota(jnp.int32, (band, band), 1)
    nd_own = jnp.where(kj <= qi, ((kj - qi) * dil).astype(F32), NEG_INF)
    nd_prev = jnp.where((kj >= qi) & (n > 0), ((kj - qi - band) * dil).astype(F32), NEG_INF)
    heads = range(len(slopes))
    head_cols = [slice(h * head_dim, (h + 1) * head_dim) for h in heads]
    scores = []
    for h in heads:
        q = q_ref[0, 0, :, head_cols[h]]
        scores.append((_dot_nt(q, ko_ref[0, 0, :, head_cols[h]]) * scale + slopes[h] * nd_own,
                       _dot_nt(q, kp_ref[0, 0, :, head_cols[h]]) * scale + slopes[h] * nd_prev))
    probs = []
    for h, (s_o, s_p) in zip(heads, scores):
        m = jnp.maximum(jnp.max(s_o, axis=-1, keepdims=True), jnp.max(s_p, axis=-1, keepdims=True))
        if chained:
            lse_prev = lin_ref[0, h, rows, :]
            m = jnp.maximum(m, lse_prev)
        p_o = jnp.exp(s_o - m)
        p_p = jnp.exp(s_p - m)
        l = jnp.sum(p_o, axis=-1, keepdims=True) + jnp.sum(p_p, axis=-1, keepdims=True)
        w_prev = None
        if chained:
            w_prev = jnp.exp(lse_prev - m)
            l = l + w_prev
        if not last:
            l_ref[0, h, rows, :] = jnp.broadcast_to(m + jnp.log(l), (band, head_dim))
        probs.append((p_o.astype(BF16), p_p.astype(BF16), l, w_prev))
    for h, (p_o, p_p, l, w_prev) in zip(heads, probs):
        acc = _dot(p_o, vo_ref[0, 0, :, head_cols[h]]) + _dot(p_p, vp_ref[0, 0, :, head_cols[h]])
        if chained:
            acc = acc + w_prev * oin_ref[0, h, rows, :]
        o_ref[0, h, rows, :] = acc / l


def _dil_attn(qkv, g, state):
    b, dil, n_sub, ncol = qkv.shape
    width = ncol // 3
    head_dim = width // DIL_HEADS
    s = n_sub * dil
    n_blk = n_sub // DIL_BAND
    groups = len(DIL_PATTERNS)
    last = g == groups - 1
    slopes = _alibi_slopes(groups * DIL_HEADS)[g * DIL_HEADS:(g + 1) * DIL_HEADS]

    def spec(c, prev):
        return pl.BlockSpec((1, 1, DIL_BAND, width),
                            lambda i, n, r: (i, r, (jnp.maximum(n - 1, 0) if prev else n), c))

    rows = DIL_BAND * dil
    o_spec = pl.BlockSpec((1, DIL_HEADS, rows, head_dim), lambda i, n, r: (i, 0, n, 0))
    in_specs = [spec(0, False), spec(1, True), spec(1, False), spec(2, True), spec(2, False)]
    args = [qkv] * 5
    if state is not None:
        in_specs += [o_spec, o_spec]
        args += list(state)
    out_specs = [o_spec] if last else [o_spec, o_spec]
    out_shape = [jax.ShapeDtypeStruct((b, DIL_HEADS, s, head_dim), F32)] * (1 if last else 2)
    return pl.pallas_call(
        functools.partial(_dil_attn_kernel, dil=dil, slopes=slopes, head_dim=head_dim,
                          chained=state is not None, last=last),
        grid=(b, n_blk, dil),
        in_specs=in_specs,
        out_specs=out_specs,
        out_shape=out_shape,
        compiler_params=_params("arbitrary", "arbitrary", "arbitrary"),
        name=f"dil_attn{g}",
    )(*args)


def _dil_decode_kernel(q_ref, kvn_ref, slope_ref, c0_ref, c1_ref, c2_ref, o_ref, *, head_dim):
    scale = head_dim ** -0.5
    steps = (DIL_BAND - lax.broadcasted_iota(jnp.int32, (DIL_BAND, 1, 1), 0)).astype(F32)
    outs, lses = [], []
    for g, c_ref in enumerate((c0_ref, c1_ref, c2_ref)):
        dil = DIL_PATTERNS[g][1]
        q = q_ref[0, g]
        kn = _as_dot_operand(kvn_ref[0, 2 * g])
        vn = _as_dot_operand(kvn_ref[0, 2 * g + 1])
        kc = _as_dot_operand(c_ref[0, :, 0, 0])
        vc = _as_dot_operand(c_ref[0, :, 0, 1])
        slope = slope_ref[g][:, 0:1]
        s = jnp.sum(kc * q[None], axis=-1, keepdims=True) * scale - slope[None] * (steps * float(dil))
        s_new = jnp.sum(kn * q, axis=-1, keepdims=True) * scale
        m = jnp.maximum(jnp.max(s, axis=0), s_new)
        p = jnp.exp(s - m[None])
        p_new = jnp.exp(s_new - m)
        l = jnp.sum(p, axis=0) + p_new
        acc = jnp.sum(_as_dot_operand(p) * vc, axis=0) + _as_dot_operand(p_new) * vn
        outs.append(acc / l)
        lses.append(m + jnp.log(l))
    m = jnp.maximum(jnp.maximum(lses[0], lses[1]), lses[2])
    es = [jnp.exp(l - m) for l in lses]
    o_ref[0] = (es[0] * outs[0] + es[1] * outs[1] + es[2] * outs[2]) / (es[0] + es[1] + es[2])


def _dil_decode(q, kv_new, caches):
    n, _, heads, head_dim = q.shape
    views, specs = [], []
    for g, (win, dil) in enumerate(DIL_PATTERNS):
        assert caches[g].shape[1] == win, "the window buffers must be full"
        views.append(caches[g].reshape(n, DIL_BAND, dil, 2, heads, head_dim))
        specs.append(pl.BlockSpec((1, DIL_BAND, 1, 2, heads, head_dim), lambda i: (i, 0, 0, 0, 0, 0)))
    slopes = jnp.asarray(_alibi_slopes(len(DIL_PATTERNS) * heads), F32).reshape(len(DIL_PATTERNS), heads, 1)
    slopes = jnp.broadcast_to(slopes, (len(DIL_PATTERNS), heads, head_dim))
    return pl.pallas_call(
        functools.partial(_dil_decode_kernel, head_dim=head_dim),
        grid=(n,),
        in_specs=[pl.BlockSpec((1,) + q.shape[1:], lambda i: (i, 0, 0, 0)),
                  pl.BlockSpec((1,) + kv_new.shape[1:], lambda i: (i, 0, 0, 0)),
                  _resident(slopes.shape)] + specs,
        out_specs=pl.BlockSpec((1, heads, head_dim), lambda i: (i, 0, 0)),
        out_shape=jax.ShapeDtypeStruct((n, heads, head_dim), F32),
        compiler_params=_params("arbitrary"),
        name="dil_decode",
    )(q, kv_new, slopes, *views)


def _dil_layer(yp, ys, caches, g_mix, w_qkv, q_gain, k_gain, w_o):
    b, s, d = yp.shape
    n = ys.shape[0]
    groups = len(DIL_PATTERNS)
    width = w_qkv.shape[1] // (3 * groups)
    head_dim = width // DIL_HEADS
    w16 = w_qkv.astype(BF16)
    wo16 = w_o.astype(BF16)
    ones = jnp.ones((width,), F32)
    gains = jnp.stack([row for g in range(groups)
                       for row in (_tiled_gain(q_gain[g], DIL_HEADS), _tiled_gain(k_gain[g], DIL_HEADS), ones)])
    gains = gains.reshape(3 * groups, 1, width)
    tail = min(max(w for w, _ in DIL_PATTERNS), s)
    *qkv, kv32 = _dil_proj(yp, g_mix, w16, gains, tail, [dil for _, dil in DIL_PATTERNS])
    state = None
    for g in range(groups):
        state = _dil_attn(qkv[g], g, state)
    yp_new = _out_proj_heads(state[0], wo16, yp)
    rows_p = [kv32[:, tail - min(w, s):, g * 2 * width:(g + 1) * 2 * width].reshape(b, min(w, s), 2, DIL_HEADS, head_dim)
              for g, (w, _) in enumerate(DIL_PATTERNS)]
    *qkv_s, kvs32 = _dil_proj(ys[None], g_mix, w16, gains, n, [1] * groups)
    q_s = jnp.stack([a[0, 0, :, :width] for a in qkv_s], axis=1).astype(F32).reshape(n, groups, DIL_HEADS, head_dim)
    att = _dil_decode(q_s, kvs32.reshape(n, 2 * groups, DIL_HEADS, head_dim), caches)
    ys_new = _out_proj(att.reshape(n, width), wo16, ys)
    kvs = kvs32.reshape(n, 1, groups, 2, DIL_HEADS, head_dim)
    rows_s = [kvs[:, :, g] for g in range(groups)]
    return yp_new, ys_new, rows_p, rows_s


def _top_blocks(gate, block_idx, axis):
    n = gate.shape[axis]
    chosen = jnp.zeros(gate.shape, jnp.bool_)
    for _ in range(MOBA_TOPK):
        mx = jnp.max(gate, axis=axis, keepdims=True)
        idx = jnp.min(jnp.where(gate == mx, block_idx, n), axis=axis, keepdims=True)
        hit = (block_idx == idx) & (mx > NEG_INF)
        chosen = chosen | hit
        gate = jnp.where(block_idx == idx, NEG_INF, gate)
    return chosen


def _moba_attn_kernel(ti_ref, tj_ref, q_ref, kd_ref, vtd_ref, k_ref, vt_ref, km_ref, wo_ref, x_ref, y_ref,
                      qm_ref, bits_ref, m_ref, l_ref, acc_ref, *, slopes, head_dim):
    step = pl.program_id(1)
    i = ti_ref[step]
    j = tj_ref[step]
    blk = MOBA_BLOCK
    lanes = 2 * head_dim
    heads = len(slopes)
    scale = head_dim ** -0.5
    lane = lax.broadcasted_iota(jnp.int32, (1, lanes), 1)
    rk = lax.broadcasted_iota(jnp.int32, (blk, blk), 0)
    rq = lax.broadcasted_iota(jnp.int32, (blk, blk), 1)

    @pl.when(j < 0)
    def _():
        nb = km_ref.shape[1]
        bidx = lax.broadcasted_iota(jnp.int32, (nb, blk), 0)
        for pair in range(heads // 2):
            cols = slice(pair * lanes, (pair + 1) * lanes)
            qp = q_ref[0, :, cols]
            kmp = km_ref[0, :, cols]
            for half in range(2):
                h = 2 * pair + half
                own = (lane < head_dim) if half == 0 else (lane >= head_dim)
                base = head_dim if half == 0 else 0
                ones = jnp.where((lane >= base) & (lane < base + ALIBI_PIECES), 1.0, 0.0)
                qm_ref[h] = jnp.where(own, qp * (scale * LOG2E), ones).astype(BF16)
                gate = _dot_nt(kmp.astype(BF16), jnp.where(own, qp, 0.0).astype(BF16))
                chosen = _top_blocks(jnp.where(bidx < i, gate, NEG_INF), bidx, 0)
                bits_ref[h] = jnp.sum(jnp.where(chosen, jnp.left_shift(1, bidx), 0), axis=0, keepdims=True)

    def own_block_step(h, s):
        s = jnp.where(rk <= rq, s, NEG_INF)
        m_new = jnp.max(s, axis=0, keepdims=True)
        p = jnp.exp2(s - m_new)
        m_ref[h] = m_new
        l_ref[h] = jnp.sum(p, axis=0, keepdims=True)
        return p.astype(BF16), None

    def past_blocks_step(h, parts):
        m_prev = m_ref[h]
        offsets, picked = [], []
        m_new = m_prev
        for sub, s in enumerate(parts):
            block = PAST_PER_STEP * j + sub
            offsets.append((slopes[h] * LOG2E) * ((block - i) * blk).astype(F32))
            picked.append((jnp.right_shift(bits_ref[h], block) & 1) == 1)
            m_new = jnp.maximum(m_new, jnp.where(picked[sub], jnp.max(s, axis=0, keepdims=True) + offsets[sub],
                                                 NEG_INF))
        alpha = jnp.exp2(m_prev - m_new)
        ps = [jnp.exp2(s - jnp.where(picked[sub], m_new - offsets[sub], MASKED_MAX)) for sub, s in enumerate(parts)]
        l_new = alpha * l_ref[h]
        for p in ps:
            l_new = l_new + jnp.sum(p, axis=0, keepdims=True)
        m_ref[h] = m_new
        l_ref[h] = l_new
        return jnp.concatenate(ps, axis=0).astype(BF16), alpha

    def attend(first):
        for h0 in range(0, heads, HEAD_GROUP):
            group = range(h0, h0 + HEAD_GROUP)
            if first:
                probs = [own_block_step(h, _dot_nt(kd_ref[0, :, h * lanes:(h + 1) * lanes], qm_ref[h]))
                         for h in group]
            else:
                scores = [[_dot_nt(k_ref[0, sub * blk:(sub + 1) * blk, h * lanes:(h + 1) * lanes], qm_ref[h])
                           for sub in range(PAST_PER_STEP)] for h in group]
                probs = [past_blocks_step(h, parts) for h, parts in zip(group, scores)]
            for h, (p, alpha) in zip(group, probs):
                rows = slice(h * head_dim, (h + 1) * head_dim)
                if first:
                    acc_ref[rows, :] = _dot(vtd_ref[0, rows, :], p)
                else:
                    acc_ref[rows, :] = alpha * acc_ref[rows, :] + _dot(vt_ref[0, rows, :], p)

    @pl.when(j < 0)
    def _():
        attend(True)

    @pl.when(j >= 0)
    def _():
        attend(False)

    @pl.when((i == 0) | (j == (i + PAST_PER_STEP - 1) // PAST_PER_STEP - 1))
    def _():
        parts = [acc_ref[h * head_dim:(h + 1) * head_dim, :] / l_ref[h] for h in range(heads)]
        out_t = jnp.concatenate(parts, axis=0).astype(BF16)
        y_ref[0] = x_ref[0] + _dot_tn(out_t, wo_ref[...])


def _moba_attn(q32, k16, vt16, km, wo16, x):
    b, s, width = q32.shape
    d = x.shape[-1]
    head_dim = width // MOBA_HEADS
    nt = s // MOBA_BLOCK
    assert nt <= 32, "the chosen (strictly earlier) blocks are kept as bits 0..30 of an int32"
    assert nt % PAST_PER_STEP == 0
    steps = [(i, j) for i in range(nt) for j in [-1, *range(-(-i // PAST_PER_STEP))]]
    ti = [i for i, _ in steps]
    tj = [j for _, j in steps]
    past = PAST_PER_STEP * MOBA_BLOCK
    q_tile = lambda n, t, ti, tj: (n, ti[t], 0)
    grid_spec = pltpu.PrefetchScalarGridSpec(
        num_scalar_prefetch=2,
        grid=(b, len(steps)),
        in_specs=[pl.BlockSpec((1, MOBA_BLOCK, width), q_tile),
                  pl.BlockSpec((1, MOBA_BLOCK, 2 * width), q_tile),
                  pl.BlockSpec((1, width, MOBA_BLOCK), lambda n, t, ti, tj: (n, 0, ti[t])),
                  pl.BlockSpec((1, past, 2 * width), lambda n, t, ti, tj: (n, jnp.maximum(tj[t], 0), 0)),
                  pl.BlockSpec((1, width, past), lambda n, t, ti, tj: (n, 0, jnp.maximum(tj[t], 0))),
                  pl.BlockSpec((1, nt, width), lambda n, t, ti, tj: (n, 0, 0)),
                  pl.BlockSpec(wo16.shape, lambda n, t, ti, tj: (0, 0), pipeline_mode=pl.Buffered(1)),
                  pl.BlockSpec((1, MOBA_BLOCK, d), q_tile)],
        out_specs=pl.BlockSpec((1, MOBA_BLOCK, d), q_tile),
        scratch_shapes=[pltpu.VMEM((MOBA_HEADS, MOBA_BLOCK, 2 * head_dim), BF16),
                        pltpu.VMEM((MOBA_HEADS, 1, MOBA_BLOCK), jnp.int32),
                        pltpu.VMEM((MOBA_HEADS, 1, MOBA_BLOCK), F32),
                        pltpu.VMEM((MOBA_HEADS, 1, MOBA_BLOCK), F32),
                        pltpu.VMEM((width, MOBA_BLOCK), F32)])
    return pl.pallas_call(
        functools.partial(_moba_attn_kernel, slopes=_alibi_slopes(MOBA_HEADS), head_dim=head_dim),
        grid_spec=grid_spec,
        out_shape=jax.ShapeDtypeStruct((b, s, d), F32),
        compiler_params=_params("arbitrary", "arbitrary"),
        name="moba_attn",
    )(jnp.asarray(ti, jnp.int32), jnp.asarray(tj, jnp.int32), q32, k16, vt16, k16, vt16, km, wo16, x)


def _moba_decode_kernel(pt_ref, q_ref, kvn_ref, slope_ref, *refs, past_len):
    page_refs = refs[:2 * DECODE_BLOCKS]
    o_ref, gate_ref, m_ref, l_ref, acc_ref = refs[2 * DECODE_BLOCKS:]
    j = pl.program_id(1)
    heads, head_dim, keys = q_ref.shape[1:]
    scale = head_dim ** -0.5
    q = q_ref[0]
    slope = slope_ref[...]
    lane = lax.broadcasted_iota(jnp.int32, (1, 1, keys), 2)

    @pl.when(j == 0)
    def _():
        gate_ref[...] = jnp.full(gate_ref.shape, NEG_INF, F32)
        m_ref[...] = jnp.zeros(m_ref.shape, F32)
        l_ref[...] = jnp.zeros(l_ref.shape, F32)
        acc_ref[...] = jnp.zeros(acc_ref.shape, F32)

    for sub in range(DECODE_BLOCKS):
        block = j * DECODE_BLOCKS + sub
        p0_ref, p1_ref = page_refs[2 * sub:2 * sub + 2]
        scores = []
        for half, page_ref in enumerate((p0_ref, p1_ref)):
            dist = (past_len - block * MOBA_BLOCK - half * keys - lane).astype(F32)
            qk = jnp.sum(q * page_ref[0, 0], axis=1, keepdims=True)
            scores.append(qk * scale - slope * dist)
        m = jnp.maximum(jnp.max(scores[0], axis=-1, keepdims=True), jnp.max(scores[1], axis=-1, keepdims=True))
        p0 = jnp.exp(scores[0] - m)
        p1 = jnp.exp(scores[1] - m)
        here = lane == block
        m_ref[...] = jnp.where(here, m, m_ref[...])
        l_ref[...] = jnp.where(here, jnp.sum(p0 + p1, axis=-1, keepdims=True), l_ref[...])
        acc = jnp.sum(p0 * p0_ref[0, 1] + p1 * p1_ref[0, 1], axis=-1, keepdims=True)
        acc_ref[...] = jnp.where(here, acc, acc_ref[...])
        k_mean = jnp.sum(p0_ref[0, 0] + p1_ref[0, 0], axis=-1, keepdims=True) * (1.0 / MOBA_BLOCK)
        gate = jnp.sum(q[:, :, 0:1] * _as_dot_operand(k_mean), axis=1, keepdims=True)
        gate_ref[...] = jnp.where(here, gate, gate_ref[...])

    @pl.when(j == pl.num_programs(1) - 1)
    def _():
        chosen = _top_blocks(gate_ref[...], jnp.broadcast_to(lane, gate_ref.shape), 2)
        kn = _as_dot_operand(kvn_ref[0, 0])
        vn = _as_dot_operand(kvn_ref[0, 1])
        s_new = jnp.sum(q * kn, axis=1, keepdims=True)[:, :, 0:1] * scale
        m_all = m_ref[...]
        m_tot = jnp.maximum(jnp.max(jnp.where(chosen, m_all, NEG_INF), axis=-1, keepdims=True), s_new)
        w = jnp.where(chosen, jnp.exp(m_all - m_tot), 0.0)
        e_new = jnp.exp(s_new - m_tot)
        den = jnp.sum(w * l_ref[...], axis=-1, keepdims=True) + e_new
        o_ref[0] = (jnp.sum(w * acc_ref[...], axis=-1, keepdims=True) + _as_dot_operand(e_new) * vn) / den


def _moba_decode(q, kv_new, cache, page_table):
    n, heads, head_dim = q.shape
    n_pages = page_table.shape[1]
    per_blk = MOBA_BLOCK // PAGE_SIZE
    assert cache.shape[1] == PAGE_SIZE and per_blk == 2 and n_pages % (per_blk * DECODE_BLOCKS) == 0
    nb = n_pages // per_blk
    assert nb <= PAGE_SIZE, "per-block statistics are kept one block per lane"
    pages_t = jnp.transpose(cache, (0, 2, 3, 4, 1))
    q_b = jnp.broadcast_to(_as_dot_operand(q)[..., None], (n, heads, head_dim, PAGE_SIZE))
    slopes = jnp.broadcast_to(jnp.asarray(_alibi_slopes(heads), F32)[:, None, None], (heads, 1, PAGE_SIZE))
    row4 = lambda i, j, pt: (i, 0, 0, 0)
    row5 = lambda i, j, pt: (i, 0, 0, 0, 0)
    pages_per_step = per_blk * DECODE_BLOCKS

    def page_spec(k):
        return pl.BlockSpec((1, 2, heads, head_dim, PAGE_SIZE),
                            lambda i, j, pt: (pt[i, pages_per_step * j + k], 0, 0, 0, 0))

    grid_spec = pltpu.PrefetchScalarGridSpec(
        num_scalar_prefetch=1,
        grid=(n, n_pages // pages_per_step),
        in_specs=[pl.BlockSpec((1, heads, head_dim, PAGE_SIZE), row4),
                  pl.BlockSpec((1, 2, heads, head_dim, 1), row5),
                  pl.BlockSpec(slopes.shape, lambda i, j, pt: (0, 0, 0))]
        + [page_spec(k) for k in range(pages_per_step)],
        out_specs=pl.BlockSpec((1, heads, head_dim, 1), row4),
        scratch_shapes=[pltpu.VMEM((heads, 1, PAGE_SIZE), F32),
                        pltpu.VMEM((heads, 1, PAGE_SIZE), F32),
                        pltpu.VMEM((heads, 1, PAGE_SIZE), F32),
                        pltpu.VMEM((heads, head_dim, PAGE_SIZE), F32)])
    o = pl.pallas_call(
        functools.partial(_moba_decode_kernel, past_len=n_pages * PAGE_SIZE),
        grid_spec=grid_spec,
        out_shape=jax.ShapeDtypeStruct((n, heads, head_dim, 1), F32),
        compiler_params=_params("arbitrary", "arbitrary"),
        name="moba_decode",
    )(page_table, q_b, kv_new[..., None], slopes, *([pages_t] * pages_per_step))
    return o.reshape(n, heads, head_dim)


def _moba_layer(yp, ys, cache, page_table, g_mix, w_qkv, q_gain, k_gain, w_o):
    b, s, d = yp.shape
    n = ys.shape[0]
    width = w_qkv.shape[1] // 3
    head_dim = width // MOBA_HEADS
    w16 = w_qkv.astype(BF16)
    wo16 = w_o.astype(BF16)
    gains = jnp.stack([_tiled_gain(q_gain, MOBA_HEADS), _tiled_gain(k_gain, MOBA_HEADS), jnp.ones((width,), F32)])
    gains = gains.reshape(3, 1, width)
    q32, kv32, k16, vt16, km = _moba_proj(yp, g_mix, w16, gains, prompt=True)
    yp_new = _moba_attn(q32, k16, vt16, km.reshape(b, s // MOBA_BLOCK, width), wo16, yp)
    rows_p = kv32.reshape(b, s, 2, MOBA_HEADS, head_dim)
    qs32, kvs32 = _moba_proj(ys[None], g_mix, w16, gains, prompt=False)
    att_s = _moba_decode(qs32.reshape(n, MOBA_HEADS, head_dim), kvs32.reshape(n, 2, MOBA_HEADS, head_dim),
                         cache, page_table)
    ys_new = _out_proj(att_s.reshape(n, width), wo16, ys)
    rows_s = kvs32.reshape(n, 1, 2, MOBA_HEADS, head_dim)
    return yp_new, ys_new, rows_p, rows_s


def kernel(x_prompt, x_sample, state_pool, cache_dil0, cache_dil1, cache_dil2, state_conv, cache_moba, page_table,
           norm_mix, norm_ffn, pool_w, pool_scale, dil_w_qkv, dil_q_gain, dil_k_gain, dil_w_o,
           moba_w_qkv, moba_q_gain, moba_k_gain, moba_w_o, ffn_w_in, ffn_conv_w, ffn_conv_b, ffn_w_out):
    assert x_sample.shape[1] == 1, "the sample group decodes one token per sequence"
    depth = norm_mix.shape[0]
    yp, ys = x_prompt, x_sample[:, 0]
    pool_p, pool_s, moba_p, moba_s, conv_p, conv_s = [], [], [], [], [], []
    dil_p = [[] for _ in DIL_PATTERNS]
    dil_s = [[] for _ in DIL_PATTERNS]
    for layer in range(depth):
        kind, j = layer % N_MIXERS, layer // N_MIXERS
        g_mix = norm_mix[layer][None]
        if kind == 0:
            w16 = pool_w[j].astype(BF16)
            scale = pool_scale[j][None]
            yp, tail = _pool_prompt(yp, g_mix, w16, scale)
            ys, hs = _pool_sample(ys, state_pool[j].transpose(1, 0, 2), g_mix, w16, scale)
            pool_p.append(tail[:, POOL_CARRY - POOL_HIST:])
            pool_s.append(jnp.concatenate([state_pool[j][:, 1:], hs[:, None]], axis=1))
        elif kind == 1:
            caches = [c[j] for c in (cache_dil0, cache_dil1, cache_dil2)]
            yp, ys, rows_p, rows_s = _dil_layer(yp, ys, caches, g_mix, dil_w_qkv[j], dil_q_gain[j], dil_k_gain[j],
                                                dil_w_o[j])
            for g in range(len(DIL_PATTERNS)):
                dil_p[g].append(rows_p[g])
                dil_s[g].append(rows_s[g])
        else:
            yp, ys, rows_p, rows_s = _moba_layer(yp, ys, cache_moba[j], page_table, g_mix, moba_w_qkv[j],
                                                 moba_q_gain[j], moba_k_gain[j], moba_w_o[j])
            moba_p.append(rows_p)
            moba_s.append(rows_s)
        g_ffn = norm_ffn[layer][None]
        win16 = ffn_w_in[layer].astype(BF16)
        wout16 = ffn_w_out[layer].astype(BF16)
        conv_w, conv_b = ffn_conv_w[layer], ffn_conv_b[layer][None]
        yp, tail = _ffn_prompt(yp, g_ffn, win16, conv_w, conv_b, wout16)
        ys, ua, ug = _ffn_sample(ys, g_ffn, win16, state_conv[layer].transpose(1, 0, 2), conv_w, conv_b, wout16)
        conv_p.append(tail[:, CONV_CARRY - CONV_HIST:])
        u_new = jnp.concatenate([ua, ug], axis=-1)[:, None]
        conv_s.append(jnp.concatenate([state_conv[layer][:, 1:], u_new], axis=1))
    stack = lambda xs: jnp.stack(xs, axis=0)
    return (yp, ys[:, None], stack(pool_p), stack(pool_s),
            stack(dil_p[0]), stack(dil_p[1]), stack(dil_p[2]),
            stack(dil_s[0]), stack(dil_s[1]), stack(dil_s[2]),
            stack(moba_p), stack(moba_s), stack(conv_p), stack(conv_s))
```

```python
import functools

import jax
import jax.numpy as jnp
from jax import lax
from jax.experimental import pallas as pl
from jax.experimental.pallas import tpu as pltpu

F32 = jnp.float32
BF16 = jnp.bfloat16

N_MIXERS = 3
POOL_WINDOWS = (2, 4, 8, 16)
POOL_HIST = max(POOL_WINDOWS) - 1
POOL_CARRY = 16
DIL_PATTERNS = ((128, 1), (512, 4), (2048, 16))
DIL_BAND = 128
DIL_HEADS = 8
MOBA_BLOCK = 256
MOBA_TOPK = 3
MOBA_HEADS = 16
HEAD_GROUP = 16
PAST_PER_STEP = 2
ONES_ROWS = 16
DECODE_BLOCKS = 4
PAGE_SIZE = 128
CONV_WIDTH = 3
CONV_HIST = CONV_WIDTH - 1
CONV_CARRY = 8
RMS_EPS = 1e-6
LOG2E = 1.4426950408889634
ALIBI_PIECES = 3
NEG_INF = float("-inf")
MASKED_MAX = 1e30
VMEM_LIMIT = 56 * 1024 * 1024


def _params(*sem):
    return pltpu.CompilerParams(dimension_semantics=sem, vmem_limit_bytes=VMEM_LIMIT)


def _resident(shape):
    zeros = (0,) * len(shape)
    return pl.BlockSpec(shape, lambda *_: zeros, pipeline_mode=pl.Buffered(1))


def _rms(x, g):
    return x * lax.rsqrt(jnp.mean(x * x, axis=-1, keepdims=True) + RMS_EPS) * g


def _dot(a, b):
    return jnp.dot(a, b, preferred_element_type=F32)


def _dot_nt(a, b, precision=None):
    return lax.dot_general(a, b, (((1,), (1,)), ((), ())), preferred_element_type=F32, precision=precision)


def _as_dot_operand(x):
    return x.astype(BF16).astype(F32)


def _dot_tn(a, b):
    return lax.dot_general(a, b, (((0,), (0,)), ((), ())), preferred_element_type=F32)


def _alibi_slopes(n):
    return [2.0 ** (-8.0 * i / n) for i in range(1, n + 1)]


def _pool_prompt_kernel(x_ref, g_ref, w_ref, sc_ref, y_ref, tail_ref, ext_ref, *, tile):
    t = pl.program_id(1)
    d_model = x_ref.shape[-1]
    gdim = d_model // len(POOL_WINDOWS)

    @pl.when(t == 0)
    def _():
        ext_ref[0:POOL_CARRY, :] = jnp.zeros((POOL_CARRY, d_model), F32)

    x = x_ref[0]
    ext_ref[POOL_CARRY:POOL_CARRY + tile, :] = _rms(x, g_ref[...])
    row = t * tile + lax.broadcasted_iota(jnp.int32, (tile, 1), 0)
    for gi, win in enumerate(POOL_WINDOWS):
        cols = slice(gi * gdim, (gi + 1) * gdim)
        cur = ext_ref[POOL_CARRY:POOL_CARRY + tile, cols]
        acc = cur
        for j in range(1, win):
            acc = acc + ext_ref[POOL_CARRY - j:POOL_CARRY - j + tile, cols]
        cnt = jnp.minimum(row + 1, win).astype(F32)
        d = acc / cnt - cur
        y = _dot(d.astype(BF16), w_ref[gi])
        y_ref[0, :, cols] = x[:, cols] + y * sc_ref[:, cols]
    last = ext_ref[tile:tile + POOL_CARRY, :]
    ext_ref[0:POOL_CARRY, :] = last

    @pl.when(t == pl.num_programs(1) - 1)
    def _():
        tail_ref[0] = last


def _pool_prompt(x, g, w16, sc, tile=512):
    b, s, d = x.shape
    tile = min(tile, s)
    return pl.pallas_call(
        functools.partial(_pool_prompt_kernel, tile=tile),
        grid=(b, s // tile),
        in_specs=[pl.BlockSpec((1, tile, d), lambda i, t: (i, t, 0)),
                  _resident((1, d)), _resident(w16.shape), _resident((1, d))],
        out_specs=[pl.BlockSpec((1, tile, d), lambda i, t: (i, t, 0)),
                   pl.BlockSpec((1, POOL_CARRY, d), lambda i, t: (i, 0, 0))],
        out_shape=[jax.ShapeDtypeStruct((b, s, d), F32), jax.ShapeDtypeStruct((b, POOL_CARRY, d), F32)],
        scratch_shapes=[pltpu.VMEM((POOL_CARRY + tile, d), F32)],
        compiler_params=_params("arbitrary", "arbitrary"),
        name="pool_prompt",
    )(x, g, w16, sc)


def _pool_sample_kernel(x_ref, st_ref, g_ref, w_ref, sc_ref, y_ref, h_ref):
    d_model = x_ref.shape[-1]
    gdim = d_model // len(POOL_WINDOWS)
    x = x_ref[...]
    h = _rms(x, g_ref[...])
    h_ref[...] = h
    for gi, win in enumerate(POOL_WINDOWS):
        cols = slice(gi * gdim, (gi + 1) * gdim)
        cur = h[:, cols]
        acc = cur
        for j in range(1, win):
            acc = acc + st_ref[POOL_HIST - j, :, cols]
        d = acc / float(win) - cur
        y = _dot(d.astype(BF16), w_ref[gi])
        y_ref[:, cols] = x[:, cols] + y * sc_ref[:, cols]


def _pool_sample(x, state_t, g, w16, sc):
    n, d = x.shape
    return pl.pallas_call(
        _pool_sample_kernel,
        out_shape=[jax.ShapeDtypeStruct((n, d), F32), jax.ShapeDtypeStruct((n, d), F32)],
        compiler_params=pltpu.CompilerParams(vmem_limit_bytes=VMEM_LIMIT),
        name="pool_sample",
    )(x, state_t, g, w16, sc)


def _silu(g):
    return g * (1.0 / (1.0 + jnp.exp(-g)))


def _ffn_prompt_kernel(y_ref, g_ref, win_ref, cw_ref, cb_ref, wout_ref, o_ref, tail_ref,
                       u_even_ref, u_odd_ref, carry_ref, acc_ref, *, tile, chunk):
    t = pl.program_id(1)
    hidden = wout_ref.shape[0]
    u_bufs = (u_even_ref, u_odd_ref)

    @pl.when(t == 0)
    def _():
        carry_ref[...] = jnp.zeros(carry_ref.shape, F32)

    y = y_ref[0]
    h = _rms(y, g_ref[...]).astype(BF16)
    n_chunks = hidden // chunk

    def up_project(c):
        us = []
        for part in range(2):
            cols = slice(part * hidden + c * chunk, part * hidden + (c + 1) * chunk)
            u = _dot(h, win_ref[:, cols])
            u_bufs[c % 2][part, 0:CONV_CARRY, :] = carry_ref[:, cols]
            u_bufs[c % 2][part, CONV_CARRY:CONV_CARRY + tile, :] = u
            carry_ref[:, cols] = u[tile - CONV_CARRY:tile, :]
            us.append(u)
        return us

    def conv_gate(c, us):
        conv = []
        for part in range(2):
            cols = slice(part * hidden + c * chunk, part * hidden + (c + 1) * chunk)
            cv = cb_ref[:, cols] + cw_ref[2:3, cols] * us[part]
            cv = cv + cw_ref[1:2, cols] * u_bufs[c % 2][part, CONV_CARRY - 1:CONV_CARRY - 1 + tile, :]
            cv = cv + cw_ref[0:1, cols] * u_bufs[c % 2][part, CONV_CARRY - 2:CONV_CARRY - 2 + tile, :]
            conv.append(cv)
        return (_silu(conv[1]) * conv[0]).astype(BF16)

    us_next = up_project(0)
    for c in range(n_chunks):
        us = us_next
        if c + 1 < n_chunks:
            us_next = up_project(c + 1)
        contrib = _dot(conv_gate(c, us), wout_ref[c * chunk:(c + 1) * chunk, :])
        if c == 0:
            acc_ref[...] = contrib
        else:
            acc_ref[...] += contrib
    o_ref[0] = y + acc_ref[...]

    @pl.when(t == pl.num_programs(1) - 1)
    def _():
        tail_ref[0] = carry_ref[...]


def _ffn_prompt(y, g, win16, cw, cb, wout16, tile=512, chunk=256):
    b, s, d = y.shape
    hidden = wout16.shape[0]
    tile = min(tile, s)
    return pl.pallas_call(
        functools.partial(_ffn_prompt_kernel, tile=tile, chunk=chunk),
        grid=(b, s // tile),
        in_specs=[pl.BlockSpec((1, tile, d), lambda i, t: (i, t, 0)),
                  _resident((1, d)), _resident(win16.shape), _resident(cw.shape), _resident(cb.shape),
                  _resident(wout16.shape)],
        out_specs=[pl.BlockSpec((1, tile, d), lambda i, t: (i, t, 0)),
                   pl.BlockSpec((1, CONV_CARRY, 2 * hidden), lambda i, t: (i, 0, 0))],
        out_shape=[jax.ShapeDtypeStruct((b, s, d), F32),
                   jax.ShapeDtypeStruct((b, CONV_CARRY, 2 * hidden), F32)],
        scratch_shapes=[pltpu.VMEM((2, CONV_CARRY + tile, chunk), F32),
                        pltpu.VMEM((2, CONV_CARRY + tile, chunk), F32),
                        pltpu.VMEM((CONV_CARRY, 2 * hidden), F32),
                        pltpu.VMEM((tile, d), F32)],
        compiler_params=_params("arbitrary", "arbitrary"),
        name="ffn_prompt",
    )(y, g, win16, cw, cb, wout16)


def _ffn_sample_kernel(y_ref, g_ref, wa_ref, wg_ref, ha_ref, hg_ref, cwa_ref, cwg_ref, cba_ref, cbg_ref,
                       wout_ref, o_ref, ua_ref, ug_ref):
    c = pl.program_id(0)
    y = y_ref[...]
    h = _rms(y, g_ref[...]).astype(BF16)

    def conv(w_ref, hist_ref, cw_ref, cb_ref, u_out_ref):
        u = _dot(h, w_ref[...])
        u_out_ref[...] = u
        return cb_ref[...] + cw_ref[0:1, :] * hist_ref[0] + cw_ref[1:2, :] * hist_ref[1] + cw_ref[2:3, :] * u

    ca = conv(wa_ref, ha_ref, cwa_ref, cba_ref, ua_ref)
    cg = conv(wg_ref, hg_ref, cwg_ref, cbg_ref, ug_ref)
    contrib = _dot((_silu(cg) * ca).astype(BF16), wout_ref[...])

    @pl.when(c == 0)
    def _():
        o_ref[...] = y + contrib

    @pl.when(c != 0)
    def _():
        o_ref[...] += contrib


def _ffn_sample(y, g, win16, hist_t, cw, cb, wout16, chunk=256):
    n, d = y.shape
    hidden = wout16.shape[0]
    nc = hidden // chunk
    a_col = lambda c: (0, c)
    g_col = lambda c: (0, c + nc)
    return pl.pallas_call(
        _ffn_sample_kernel,
        grid=(nc,),
        in_specs=[_resident((n, d)), _resident((1, d)),
                  pl.BlockSpec((d, chunk), a_col), pl.BlockSpec((d, chunk), g_col),
                  pl.BlockSpec((CONV_HIST, n, chunk), lambda c: (0, 0, c)),
                  pl.BlockSpec((CONV_HIST, n, chunk), lambda c: (0, 0, c + nc)),
                  pl.BlockSpec((CONV_WIDTH, chunk), a_col), pl.BlockSpec((CONV_WIDTH, chunk), g_col),
                  pl.BlockSpec((1, chunk), a_col), pl.BlockSpec((1, chunk), g_col),
                  pl.BlockSpec((chunk, d), lambda c: (c, 0))],
        out_specs=[pl.BlockSpec((n, d), lambda c: (0, 0)),
                   pl.BlockSpec((n, chunk), a_col), pl.BlockSpec((n, chunk), a_col)],
        out_shape=[jax.ShapeDtypeStruct((n, d), F32), jax.ShapeDtypeStruct((n, hidden), F32),
                   jax.ShapeDtypeStruct((n, hidden), F32)],
        compiler_params=_params("arbitrary"),
        name="ffn_sample",
    )(y, g, win16, win16, hist_t, hist_t, cw, cw, cb, cb, wout16)


def _out_proj_kernel(a_ref, w_ref, x_ref, y_ref):
    y_ref[...] = x_ref[...] + _dot(a_ref[...].astype(BF16), w_ref[...])


def _out_proj(a, w16, x, tile=512):
    n, d = x.shape
    tile = min(tile, n)
    return pl.pallas_call(
        _out_proj_kernel,
        grid=(n // tile,),
        in_specs=[pl.BlockSpec((tile, a.shape[1]), lambda t: (t, 0)), _resident(w16.shape),
                  pl.BlockSpec((tile, d), lambda t: (t, 0))],
        out_specs=pl.BlockSpec((tile, d), lambda t: (t, 0)),
        out_shape=jax.ShapeDtypeStruct((n, d), F32),
        compiler_params=_params("arbitrary"),
        name="out_proj",
    )(a, w16, x)


def _out_proj_heads_kernel(a_ref, w_ref, x_ref, y_ref):
    a = jnp.concatenate([a_ref[0, h] for h in range(a_ref.shape[1])], axis=1)
    y_ref[0] = x_ref[0] + _dot(a.astype(BF16), w_ref[...])


def _out_proj_heads(a, w16, x, tile=512):
    b, heads, s, head_dim = a.shape
    d = x.shape[-1]
    tile = min(tile, s)
    return pl.pallas_call(
        _out_proj_heads_kernel,
        grid=(b, s // tile),
        in_specs=[pl.BlockSpec((1, heads, tile, head_dim), lambda i, t: (i, 0, t, 0)), _resident(w16.shape),
                  pl.BlockSpec((1, tile, d), lambda i, t: (i, t, 0))],
        out_specs=pl.BlockSpec((1, tile, d), lambda i, t: (i, t, 0)),
        out_shape=jax.ShapeDtypeStruct((b, s, d), F32),
        compiler_params=_params("arbitrary", "arbitrary"),
        name="out_proj_heads",
    )(a, w16, x)


def _head_norm(r, gain, head_dim):
    lanes = 128
    low = lax.broadcasted_iota(jnp.int32, (1, lanes), 1) < head_dim
    cols = []
    for v in range(r.shape[1] // lanes):
        seg = r[:, v * lanes:(v + 1) * lanes]
        sq = seg * seg
        if head_dim == lanes:
            ms = jnp.mean(sq, axis=-1, keepdims=True)
        else:
            lo = jnp.sum(jnp.where(low, sq, 0.0), axis=-1, keepdims=True)
            hi = jnp.sum(jnp.where(low, 0.0, sq), axis=-1, keepdims=True)
            ms = jnp.where(low, lo, hi) * (1.0 / head_dim)
        cols.append(seg * lax.rsqrt(ms + RMS_EPS))
    return jnp.concatenate(cols, axis=1) * gain


def _tiled_gain(gain, heads):
    return jnp.tile(gain.astype(F32), heads)


def _dil_proj_kernel(x_ref, g_ref, w_ref, gain_ref, o0_ref, o1_ref, o2_ref, kv32_ref, h_ref, r_ref,
                     *, dils, head_dim):
    j = pl.program_id(2)
    g = j // 3
    c = j % 3
    chunks, tile, lanes = r_ref.shape

    @pl.when(j == 0)
    def _():
        h_ref[...] = _rms(x_ref[0], g_ref[...]).astype(BF16)

    r = _dot(h_ref[...], w_ref[...])

    def emit(val):
        @pl.when(c > 0)
        def _():
            kv32_ref[0] = val

        for gi, (o_ref, dil) in enumerate(zip((o0_ref, o1_ref, o2_ref), dils)):
            @pl.when(g == gi)
            def _(o_ref=o_ref, dil=dil):
                if dil == 1:
                    o_ref[0, 0] = val.astype(BF16)
                    return
                for ch in range(chunks):
                    r_ref[ch] = val[:, ch * lanes:(ch + 1) * lanes]
                for res in range(dil):
                    for ch in range(chunks):
                        o_ref[0, res, :, ch * lanes:(ch + 1) * lanes] = (
                            r_ref[ch, pl.ds(res, tile // dil, stride=dil), :].astype(BF16))

    @pl.when(c < 2)
    def _():
        emit(_head_norm(r, gain_ref[0], head_dim))

    @pl.when(c == 2)
    def _():
        emit(r)


def _dil_proj(x, g, w16, gains, tail_rows, dils, tile=1024):
    b, s, d = x.shape
    groups = len(dils)
    width = w16.shape[1] // (3 * groups)
    tile = min(tile, s, tail_rows)
    t0 = (s - tail_rows) // tile

    def kv_map(i, t, j):
        col = (j // 3) * 2 + jnp.maximum(j % 3 - 1, 0)
        return i, jnp.maximum(t - t0, 0), jnp.where(t >= t0, col, 0)

    def group_spec(gi, dil):
        return pl.BlockSpec((1, dil, tile // dil, width),
                            lambda i, t, j: (i, 0, t, jnp.clip(j - 3 * gi, 0, 2)))

    return pl.pallas_call(
        functools.partial(_dil_proj_kernel, dils=tuple(dils), head_dim=width // DIL_HEADS),
        grid=(b, s // tile, 3 * groups),
        in_specs=[pl.BlockSpec((1, tile, d), lambda i, t, j: (i, t, 0)), _resident((1, d)),
                  pl.BlockSpec((d, width), lambda i, t, j: (0, j)),
                  pl.BlockSpec((1, 1, width), lambda i, t, j: (j, 0, 0))],
        out_specs=[group_spec(gi, dil) for gi, dil in enumerate(dils)]
        + [pl.BlockSpec((1, tile, width), kv_map)],
        out_shape=[jax.ShapeDtypeStruct((b, dil, s // dil, 3 * width), BF16) for dil in dils]
        + [jax.ShapeDtypeStruct((b, tail_rows, groups * 2 * width), F32)],
        scratch_shapes=[pltpu.VMEM((tile, d), BF16), pltpu.VMEM((width // 128, tile, 128), F32)],
        compiler_params=_params("arbitrary", "arbitrary", "arbitrary"),
        name="dil_proj",
    )(x, g, w16, gains)


def _moba_proj_kernel(x_ref, g_ref, w_ref, gain_ref, *refs, head_dim, prompt):
    if prompt:
        wvt_ref, q32_ref, kv32_ref, k16_ref, vt16_ref, km_ref, h_ref = refs
    else:
        q32_ref, kv32_ref, h_ref = refs
    j = pl.program_id(2)

    @pl.when(j == 0)
    def _():
        h_ref[...] = _rms(x_ref[0], g_ref[...]).astype(BF16)

    r = _dot(h_ref[...], w_ref[...])

    @pl.when(j == 0)
    def _():
        q32_ref[0] = _head_norm(r, gain_ref[0], head_dim)

    @pl.when(j == 1)
    def _():
        rn = _head_norm(r, gain_ref[0], head_dim)
        kv32_ref[0] = rn
        if prompt:
            lanes = 2 * head_dim
            lane = lax.broadcasted_iota(jnp.int32, (1, lanes), 1)
            in_block = (lax.broadcasted_iota(jnp.int32, (rn.shape[0], 1), 0) % MOBA_BLOCK).astype(F32)
            for h, slope in enumerate(_alibi_slopes(rn.shape[1] // head_dim)):
                kp = rn[:, (h // 2) * lanes:(h // 2 + 1) * lanes]
                own = (lane < head_dim) if h % 2 == 0 else (lane >= head_dim)
                base = head_dim if h % 2 == 0 else 0
                rest = in_block * (slope * LOG2E)
                term = jnp.zeros((1, lanes), F32)
                for piece in range(ALIBI_PIECES):
                    part = rest if piece == ALIBI_PIECES - 1 else rest.astype(BF16).astype(F32)
                    term = jnp.where(lane == base + piece, part, term)
                    rest = rest - part
                k16_ref[0, :, h * lanes:(h + 1) * lanes] = jnp.where(own, kp, term).astype(BF16)
            for blk in range(rn.shape[0] // MOBA_BLOCK):
                rows = rn[blk * MOBA_BLOCK:(blk + 1) * MOBA_BLOCK, :]
                km_ref[0, blk] = jnp.mean(rows, axis=0, keepdims=True)

    @pl.when(j == 2)
    def _():
        kv32_ref[0] = r
        if prompt:
            vt16_ref[0] = _dot_nt(wvt_ref[...], h_ref[...]).astype(BF16)


def _moba_proj(x, g, w16, gains, prompt, tile=1024):
    b, s, d = x.shape
    width = w16.shape[1] // 3
    tile = min(tile, s)
    in_specs = [pl.BlockSpec((1, tile, d), lambda i, t, j: (i, t, 0)), _resident((1, d)),
                pl.BlockSpec((d, width), lambda i, t, j: (0, j)),
                pl.BlockSpec((1, 1, width), lambda i, t, j: (j, 0, 0))]
    args = [x, g, w16, gains]
    out_specs = [pl.BlockSpec((1, tile, width), lambda i, t, j: (i, t, 0)),
                 pl.BlockSpec((1, tile, width), lambda i, t, j: (i, t, jnp.maximum(j - 1, 0)))]
    out_shape = [jax.ShapeDtypeStruct((b, s, width), F32), jax.ShapeDtypeStruct((b, s, 2 * width), F32)]
    if prompt:
        in_specs.append(_resident((width, d)))
        args.append(w16[:, 2 * width:].T)
        out_specs += [pl.BlockSpec((1, tile, 2 * width), lambda i, t, j: (i, t, 0)),
                      pl.BlockSpec((1, width, tile), lambda i, t, j: (i, 0, t)),
                      pl.BlockSpec((1, tile // MOBA_BLOCK, 1, width), lambda i, t, j: (i, t, 0, 0))]
        out_shape += [jax.ShapeDtypeStruct((b, s, 2 * width), BF16), jax.ShapeDtypeStruct((b, width, s), BF16),
                      jax.ShapeDtypeStruct((b, s // MOBA_BLOCK, 1, width), F32)]
    return pl.pallas_call(
        functools.partial(_moba_proj_kernel, head_dim=width // MOBA_HEADS, prompt=prompt),
        grid=(b, s // tile, 3),
        in_specs=in_specs,
        out_specs=out_specs,
        out_shape=out_shape,
        scratch_shapes=[pltpu.VMEM((tile, d), BF16)],
        compiler_params=_params("arbitrary", "arbitrary", "arbitrary"),
        name="moba_proj",
    )(*args)


def _dil_attn_kernel(q_ref, kp_ref, ko_ref, vp_ref, vo_ref, *refs, dil, slopes, head_dim, chained, last):
    refs = list(refs)
    oin_ref, lin_ref = (refs.pop(0), refs.pop(0)) if chained else (None, None)
    o_ref = refs.pop(0)
    l_ref = None if last else refs.pop(0)
    n = pl.program_id(1)
    r = pl.program_id(2)
    band = DIL_BAND
    width = len(slopes) * head_dim
    scale = head_dim ** -0.5
    rows = slice(None) if dil == 1 else pl.ds(r, band, stride=dil)
    qi = lax.broadcasted_iota(jnp.int32, (band, band), 0)
    kj = lax.broadcasted_iota(jnp.int32, (band, band), 1)
    nd_own = jnp.where(kj <= qi, ((kj - qi) * dil).astype(F32), NEG_INF)
    nd_prev = jnp.where((kj >= qi) & (n > 0), ((kj - qi - band) * dil).astype(F32), NEG_INF)
    heads = range(len(slopes))
    head_cols = [slice(h * head_dim, (h + 1) * head_dim) for h in heads]
    scores = []
    for h in heads:
        q = q_ref[0, 0, :, head_cols[h]]
        scores.append((_dot_nt(q, ko_ref[0, 0, :, head_cols[h]]) * scale + slopes[h] * nd_own,
                       _dot_nt(q, kp_ref[0, 0, :, head_cols[h]]) * scale + slopes[h] * nd_prev))
    probs = []
    for h, (s_o, s_p) in zip(heads, scores):
        m = jnp.maximum(jnp.max(s_o, axis=-1, keepdims=True), jnp.max(s_p, axis=-1, keepdims=True))
        if chained:
            lse_prev = lin_ref[0, h, rows, :]
            m = jnp.maximum(m, lse_prev)
        p_o = jnp.exp(s_o - m)
        p_p = jnp.exp(s_p - m)
        l = jnp.sum(p_o, axis=-1, keepdims=True) + jnp.sum(p_p, axis=-1, keepdims=True)
        w_prev = None
        if chained:
            w_prev = jnp.exp(lse_prev - m)
            l = l + w_prev
        if not last:
            l_ref[0, h, rows, :] = jnp.broadcast_to(m + jnp.log(l), (band, head_dim))
        probs.append((p_o.astype(BF16), p_p.astype(BF16), l, w_prev))
    for h, (p_o, p_p, l, w_prev) in zip(heads, probs):
        acc = _dot(p_o, vo_ref[0, 0, :, head_cols[h]]) + _dot(p_p, vp_ref[0, 0, :, head_cols[h]])
        if chained:
            acc = acc + w_prev * oin_ref[0, h, rows, :]
        o_ref[0, h, rows, :] = acc / l


def _dil_attn(qkv, g, state):
    b, dil, n_sub, ncol = qkv.shape
    width = ncol // 3
    head_dim = width // DIL_HEADS
    s = n_sub * dil
    n_blk = n_sub // DIL_BAND
    groups = len(DIL_PATTERNS)
    last = g == groups - 1
    slopes = _alibi_slopes(groups * DIL_HEADS)[g * DIL_HEADS:(g + 1) * DIL_HEADS]

    def spec(c, prev):
        return pl.BlockSpec((1, 1, DIL_BAND, width),
                            lambda i, n, r: (i, r, (jnp.maximum(n - 1, 0) if prev else n), c))

    rows = DIL_BAND * dil
    o_spec = pl.BlockSpec((1, DIL_HEADS, rows, head_dim), lambda i, n, r: (i, 0, n, 0))
    in_specs = [spec(0, False), spec(1, True), spec(1, False), spec(2, True), spec(2, False)]
    args = [qkv] * 5
    if state is not None:
        in_specs += [o_spec, o_spec]
        args += list(state)
    out_specs = [o_spec] if last else [o_spec, o_spec]
    out_shape = [jax.ShapeDtypeStruct((b, DIL_HEADS, s, head_dim), F32)] * (1 if last else 2)
    return pl.pallas_call(
        functools.partial(_dil_attn_kernel, dil=dil, slopes=slopes, head_dim=head_dim,
                          chained=state is not None, last=last),
        grid=(b, n_blk, dil),
        in_specs=in_specs,
        out_specs=out_specs,
        out_shape=out_shape,
        compiler_params=_params("arbitrary", "arbitrary", "arbitrary"),
        name=f"dil_attn{g}",
    )(*args)


def _dil_decode_kernel(q_ref, kvn_ref, slope_ref, c0_ref, c1_ref, c2_ref, o_ref, *, head_dim):
    scale = head_dim ** -0.5
    steps = (DIL_BAND - lax.broadcasted_iota(jnp.int32, (DIL_BAND, 1, 1), 0)).astype(F32)
    outs, lses = [], []
    for g, c_ref in enumerate((c0_ref, c1_ref, c2_ref)):
        dil = DIL_PATTERNS[g][1]
        q = q_ref[0, g]
        kn = _as_dot_operand(kvn_ref[0, 2 * g])
        vn = _as_dot_operand(kvn_ref[0, 2 * g + 1])
        kc = _as_dot_operand(c_ref[0, :, 0, 0])
        vc = _as_dot_operand(c_ref[0, :, 0, 1])
        slope = slope_ref[g][:, 0:1]
        s = jnp.sum(kc * q[None], axis=-1, keepdims=True) * scale - slope[None] * (steps * float(dil))
        s_new = jnp.sum(kn * q, axis=-1, keepdims=True) * scale
        m = jnp.maximum(jnp.max(s, axis=0), s_new)
        p = jnp.exp(s - m[None])
        p_new = jnp.exp(s_new - m)
        l = jnp.sum(p, axis=0) + p_new
        acc = jnp.sum(_as_dot_operand(p) * vc, axis=0) + _as_dot_operand(p_new) * vn
        outs.append(acc / l)
        lses.append(m + jnp.log(l))
    m = jnp.maximum(jnp.maximum(lses[0], lses[1]), lses[2])
    es = [jnp.exp(l - m) for l in lses]
    o_ref[0] = (es[0] * outs[0] + es[1] * outs[1] + es[2] * outs[2]) / (es[0] + es[1] + es[2])


def _dil_decode(q, kv_new, caches):
    n, _, heads, head_dim = q.shape
    views, specs = [], []
    for g, (win, dil) in enumerate(DIL_PATTERNS):
        assert caches[g].shape[1] == win, "the window buffers must be full"
        views.append(caches[g].reshape(n, DIL_BAND, dil, 2, heads, head_dim))
        specs.append(pl.BlockSpec((1, DIL_BAND, 1, 2, heads, head_dim), lambda i: (i, 0, 0, 0, 0, 0)))
    slopes = jnp.asarray(_alibi_slopes(len(DIL_PATTERNS) * heads), F32).reshape(len(DIL_PATTERNS), heads, 1)
    slopes = jnp.broadcast_to(slopes, (len(DIL_PATTERNS), heads, head_dim))
    return pl.pallas_call(
        functools.partial(_dil_decode_kernel, head_dim=head_dim),
        grid=(n,),
        in_specs=[pl.BlockSpec((1,) + q.shape[1:], lambda i: (i, 0, 0, 0)),
                  pl.BlockSpec((1,) + kv_new.shape[1:], lambda i: (i, 0, 0, 0)),
                  _resident(slopes.shape)] + specs,
        out_specs=pl.BlockSpec((1, heads, head_dim), lambda i: (i, 0, 0)),
        out_shape=jax.ShapeDtypeStruct((n, heads, head_dim), F32),
        compiler_params=_params("arbitrary"),
        name="dil_decode",
    )(q, kv_new, slopes, *views)


def _dil_layer(yp, ys, caches, g_mix, w_qkv, q_gain, k_gain, w_o):
    b, s, d = yp.shape
    n = ys.shape[0]
    groups = len(DIL_PATTERNS)
    width = w_qkv.shape[1] // (3 * groups)
    head_dim = width // DIL_HEADS
    w16 = w_qkv.astype(BF16)
    wo16 = w_o.astype(BF16)
    ones = jnp.ones((width,), F32)
    gains = jnp.stack([row for g in range(groups)
                       for row in (_tiled_gain(q_gain[g], DIL_HEADS), _tiled_gain(k_gain[g], DIL_HEADS), ones)])
    gains = gains.reshape(3 * groups, 1, width)
    tail = min(max(w for w, _ in DIL_PATTERNS), s)
    *qkv, kv32 = _dil_proj(yp, g_mix, w16, gains, tail, [dil for _, dil in DIL_PATTERNS])
    state = None
    for g in range(groups):
        state = _dil_attn(qkv[g], g, state)
    yp_new = _out_proj_heads(state[0], wo16, yp)
    rows_p = [kv32[:, tail - min(w, s):, g * 2 * width:(g + 1) * 2 * width].reshape(b, min(w, s), 2, DIL_HEADS, head_dim)
              for g, (w, _) in enumerate(DIL_PATTERNS)]
    *qkv_s, kvs32 = _dil_proj(ys[None], g_mix, w16, gains, n, [1] * groups)
    q_s = jnp.stack([a[0, 0, :, :width] for a in qkv_s], axis=1).astype(F32).reshape(n, groups, DIL_HEADS, head_dim)
    att = _dil_decode(q_s, kvs32.reshape(n, 2 * groups, DIL_HEADS, head_dim), caches)
    ys_new = _out_proj(att.reshape(n, width), wo16, ys)
    kvs = kvs32.reshape(n, 1, groups, 2, DIL_HEADS, head_dim)
    rows_s = [kvs[:, :, g] for g in range(groups)]
    return yp_new, ys_new, rows_p, rows_s


def _top_blocks(gate, block_idx, axis):
    n = gate.shape[axis]
    chosen = jnp.zeros(gate.shape, jnp.bool_)
    for _ in range(MOBA_TOPK):
        mx = jnp.max(gate, axis=axis, keepdims=True)
        idx = jnp.min(jnp.where(gate == mx, block_idx, n), axis=axis, keepdims=True)
        hit = (block_idx == idx) & (mx > NEG_INF)
        chosen = chosen | hit
        gate = jnp.where(block_idx == idx, NEG_INF, gate)
    return chosen


def _moba_attn_kernel(ti_ref, tj_ref, q_ref, kd_ref, vtd_ref, k_ref, vt_ref, km_ref, wo_ref, x_ref, y_ref,
                      qm_ref, bits_ref, m_ref, l_ref, acc_ref, *, slopes, head_dim):
    step = pl.program_id(1)
    i = ti_ref[step]
    j = tj_ref[step]
    blk = MOBA_BLOCK
    lanes = 2 * head_dim
    heads = len(slopes)
    scale = head_dim ** -0.5
    lane = lax.broadcasted_iota(jnp.int32, (1, lanes), 1)
    rk = lax.broadcasted_iota(jnp.int32, (blk, blk), 0)
    rq = lax.broadcasted_iota(jnp.int32, (blk, blk), 1)

    @pl.when(j < 0)
    def _():
        nb = km_ref.shape[1]
        bidx = lax.broadcasted_iota(jnp.int32, (nb, blk), 0)
        for pair in range(heads // 2):
            cols = slice(pair * lanes, (pair + 1) * lanes)
            qp = q_ref[0, :, cols]
            kmp = km_ref[0, :, cols]
            for half in range(2):
                h = 2 * pair + half
                own = (lane < head_dim) if half == 0 else (lane >= head_dim)
                base = head_dim if half == 0 else 0
                ones = jnp.where((lane >= base) & (lane < base + ALIBI_PIECES), 1.0, 0.0)
                qm_ref[h] = jnp.where(own, qp * (scale * LOG2E), ones).astype(BF16)
                gate = _dot_nt(kmp.astype(BF16), jnp.where(own, qp, 0.0).astype(BF16))
                chosen = _top_blocks(jnp.where(bidx < i, gate, NEG_INF), bidx, 0)
                bits_ref[h] = jnp.sum(jnp.where(chosen, jnp.left_shift(1, bidx), 0), axis=0, keepdims=True)

    def own_block_step(h, s):
        s = jnp.where(rk <= rq, s, NEG_INF)
        m_new = jnp.max(s, axis=0, keepdims=True)
        m_ref[h] = m_new
        return jnp.exp2(s - m_new).astype(BF16), None

    def past_blocks_step(h, parts):
        m_prev = m_ref[h]
        offsets, picked = [], []
        m_new = m_prev
        for sub, s in enumerate(parts):
            block = PAST_PER_STEP * j + sub
            offsets.append((slopes[h] * LOG2E) * ((block - i) * blk).astype(F32))
            picked.append((jnp.right_shift(bits_ref[h], block) & 1) == 1)
            m_new = jnp.maximum(m_new, jnp.where(picked[sub], jnp.max(s, axis=0, keepdims=True) + offsets[sub],
                                                 NEG_INF))
        alpha = jnp.exp2(m_prev - m_new)
        ps = [jnp.exp2(s - jnp.where(picked[sub], m_new - offsets[sub], MASKED_MAX)).astype(BF16)
              for sub, s in enumerate(parts)]
        m_ref[h] = m_new
        return jnp.concatenate(ps, axis=0), alpha

    def attend(first):
        ones_d = jnp.ones((ONES_ROWS, blk), BF16)
        ones_p = jnp.ones((ONES_ROWS, PAST_PER_STEP * blk), BF16)
        for h0 in range(0, heads, HEAD_GROUP):
            group = range(h0, h0 + HEAD_GROUP)
            if first:
                probs = [own_block_step(h, _dot_nt(kd_ref[0, :, h * lanes:(h + 1) * lanes], qm_ref[h]))
                         for h in group]
            else:
                scores = [[_dot_nt(k_ref[0, sub * blk:(sub + 1) * blk, h * lanes:(h + 1) * lanes], qm_ref[h])
                           for sub in range(PAST_PER_STEP)] for h in group]
                probs = [past_blocks_step(h, parts) for h, parts in zip(group, scores)]
            for h, (p, alpha) in zip(group, probs):
                rows = slice(h * head_dim, (h + 1) * head_dim)
                if first:
                    pv = _dot(jnp.concatenate([vtd_ref[0, rows, :], ones_d], axis=0), p)
                    acc_ref[rows, :] = pv[0:head_dim]
                    l_ref[h] = pv[head_dim:head_dim + 1]
                else:
                    pv = _dot(jnp.concatenate([vt_ref[0, rows, :], ones_p], axis=0), p)
                    acc_ref[rows, :] = alpha * acc_ref[rows, :] + pv[0:head_dim]
                    l_ref[h] = alpha * l_ref[h] + pv[head_dim:head_dim + 1]

    @pl.when(j < 0)
    def _():
        attend(True)

    @pl.when(j >= 0)
    def _():
        attend(False)

    @pl.when((i == 0) | (j == (i + PAST_PER_STEP - 1) // PAST_PER_STEP - 1))
    def _():
        parts = [acc_ref[h * head_dim:(h + 1) * head_dim, :] / l_ref[h] for h in range(heads)]
        out_t = jnp.concatenate(parts, axis=0).astype(BF16)
        y_ref[0] = x_ref[0] + _dot_tn(out_t, wo_ref[...])


def _moba_attn(q32, k16, vt16, km, wo16, x):
    b, s, width = q32.shape
    d = x.shape[-1]
    head_dim = width // MOBA_HEADS
    nt = s // MOBA_BLOCK
    assert nt <= 32, "the chosen (strictly earlier) blocks are kept as bits 0..30 of an int32"
    assert nt % PAST_PER_STEP == 0
    steps = [(i, j) for i in range(nt) for j in [-1, *range(-(-i // PAST_PER_STEP))]]
    ti = [i for i, _ in steps]
    tj = [j for _, j in steps]
    past = PAST_PER_STEP * MOBA_BLOCK
    q_tile = lambda n, t, ti, tj: (n, ti[t], 0)
    grid_spec = pltpu.PrefetchScalarGridSpec(
        num_scalar_prefetch=2,
        grid=(b, len(steps)),
        in_specs=[pl.BlockSpec((1, MOBA_BLOCK, width), q_tile),
                  pl.BlockSpec((1, MOBA_BLOCK, 2 * width), q_tile),
                  pl.BlockSpec((1, width, MOBA_BLOCK), lambda n, t, ti, tj: (n, 0, ti[t])),
                  pl.BlockSpec((1, past, 2 * width), lambda n, t, ti, tj: (n, jnp.maximum(tj[t], 0), 0)),
                  pl.BlockSpec((1, width, past), lambda n, t, ti, tj: (n, 0, jnp.maximum(tj[t], 0))),
                  pl.BlockSpec((1, nt, width), lambda n, t, ti, tj: (n, 0, 0)),
                  pl.BlockSpec(wo16.shape, lambda n, t, ti, tj: (0, 0), pipeline_mode=pl.Buffered(1)),
                  pl.BlockSpec((1, MOBA_BLOCK, d), q_tile)],
        out_specs=pl.BlockSpec((1, MOBA_BLOCK, d), q_tile),
        scratch_shapes=[pltpu.VMEM((MOBA_HEADS, MOBA_BLOCK, 2 * head_dim), BF16),
                        pltpu.VMEM((MOBA_HEADS, 1, MOBA_BLOCK), jnp.int32),
                        pltpu.VMEM((MOBA_HEADS, 1, MOBA_BLOCK), F32),
                        pltpu.VMEM((MOBA_HEADS, 1, MOBA_BLOCK), F32),
                        pltpu.VMEM((width, MOBA_BLOCK), F32)])
    return pl.pallas_call(
        functools.partial(_moba_attn_kernel, slopes=_alibi_slopes(MOBA_HEADS), head_dim=head_dim),
        grid_spec=grid_spec,
        out_shape=jax.ShapeDtypeStruct((b, s, d), F32),
        compiler_params=_params("arbitrary", "arbitrary"),
        name="moba_attn",
    )(jnp.asarray(ti, jnp.int32), jnp.asarray(tj, jnp.int32), q32, k16, vt16, k16, vt16, km, wo16, x)


def _moba_decode_kernel(pt_ref, q_ref, kvn_ref, slope_ref, *refs, past_len):
    page_refs = refs[:2 * DECODE_BLOCKS]
    o_ref, gate_ref, m_ref, l_ref, acc_ref = refs[2 * DECODE_BLOCKS:]
    j = pl.program_id(1)
    heads, head_dim, keys = q_ref.shape[1:]
    scale = head_dim ** -0.5
    q = q_ref[0]
    slope = slope_ref[...]
    lane = lax.broadcasted_iota(jnp.int32, (1, 1, keys), 2)

    @pl.when(j == 0)
    def _():
        gate_ref[...] = jnp.full(gate_ref.shape, NEG_INF, F32)
        m_ref[...] = jnp.zeros(m_ref.shape, F32)
        l_ref[...] = jnp.zeros(l_ref.shape, F32)
        acc_ref[...] = jnp.zeros(acc_ref.shape, F32)

    for sub in range(DECODE_BLOCKS):
        block = j * DECODE_BLOCKS + sub
        p0_ref, p1_ref = page_refs[2 * sub:2 * sub + 2]
        scores = []
        for half, page_ref in enumerate((p0_ref, p1_ref)):
            dist = (past_len - block * MOBA_BLOCK - half * keys - lane).astype(F32)
            qk = jnp.sum(q * page_ref[0, 0], axis=1, keepdims=True)
            scores.append(qk * scale - slope * dist)
        m = jnp.maximum(jnp.max(scores[0], axis=-1, keepdims=True), jnp.max(scores[1], axis=-1, keepdims=True))
        p0 = jnp.exp(scores[0] - m)
        p1 = jnp.exp(scores[1] - m)
        here = lane == block
        m_ref[...] = jnp.where(here, m, m_ref[...])
        l_ref[...] = jnp.where(here, jnp.sum(p0 + p1, axis=-1, keepdims=True), l_ref[...])
        acc = jnp.sum(p0 * p0_ref[0, 1] + p1 * p1_ref[0, 1], axis=-1, keepdims=True)
        acc_ref[...] = jnp.where(here, acc, acc_ref[...])
        k_mean = jnp.sum(p0_ref[0, 0] + p1_ref[0, 0], axis=-1, keepdims=True) * (1.0 / MOBA_BLOCK)
        gate = jnp.sum(q[:, :, 0:1] * _as_dot_operand(k_mean), axis=1, keepdims=True)
        gate_ref[...] = jnp.where(here, gate, gate_ref[...])

    @pl.when(j == pl.num_programs(1) - 1)
    def _():
        chosen = _top_blocks(gate_ref[...], jnp.broadcast_to(lane, gate_ref.shape), 2)
        kn = _as_dot_operand(kvn_ref[0, 0])
        vn = _as_dot_operand(kvn_ref[0, 1])
        s_new = jnp.sum(q * kn, axis=1, keepdims=True)[:, :, 0:1] * scale
        m_all = m_ref[...]
        m_tot = jnp.maximum(jnp.max(jnp.where(chosen, m_all, NEG_INF), axis=-1, keepdims=True), s_new)
        w = jnp.where(chosen, jnp.exp(m_all - m_tot), 0.0)
        e_new = jnp.exp(s_new - m_tot)
        den = jnp.sum(w * l_ref[...], axis=-1, keepdims=True) + e_new
        o_ref[0] = (jnp.sum(w * acc_ref[...], axis=-1, keepdims=True) + _as_dot_operand(e_new) * vn) / den


def _moba_decode(q, kv_new, cache, page_table):
    n, heads, head_dim = q.shape
    n_pages = page_table.shape[1]
    per_blk = MOBA_BLOCK // PAGE_SIZE
    assert cache.shape[1] == PAGE_SIZE and per_blk == 2 and n_pages % (per_blk * DECODE_BLOCKS) == 0
    nb = n_pages // per_blk
    assert nb <= PAGE_SIZE, "per-block statistics are kept one block per lane"
    pages_t = jnp.transpose(cache, (0, 2, 3, 4, 1))
    q_b = jnp.broadcast_to(_as_dot_operand(q)[..., None], (n, heads, head_dim, PAGE_SIZE))
    slopes = jnp.broadcast_to(jnp.asarray(_alibi_slopes(heads), F32)[:, None, None], (heads, 1, PAGE_SIZE))
    row4 = lambda i, j, pt: (i, 0, 0, 0)
    row5 = lambda i, j, pt: (i, 0, 0, 0, 0)
    pages_per_step = per_blk * DECODE_BLOCKS

    def page_spec(k):
        return pl.BlockSpec((1, 2, heads, head_dim, PAGE_SIZE),
                            lambda i, j, pt: (pt[i, pages_per_step * j + k], 0, 0, 0, 0))

    grid_spec = pltpu.PrefetchScalarGridSpec(
        num_scalar_prefetch=1,
        grid=(n, n_pages // pages_per_step),
        in_specs=[pl.BlockSpec((1, heads, head_dim, PAGE_SIZE), row4),
                  pl.BlockSpec((1, 2, heads, head_dim, 1), row5),
                  pl.BlockSpec(slopes.shape, lambda i, j, pt: (0, 0, 0))]
        + [page_spec(k) for k in range(pages_per_step)],
        out_specs=pl.BlockSpec((1, heads, head_dim, 1), row4),
        scratch_shapes=[pltpu.VMEM((heads, 1, PAGE_SIZE), F32),
                        pltpu.VMEM((heads, 1, PAGE_SIZE), F32),
                        pltpu.VMEM((heads, 1, PAGE_SIZE), F32),
                        pltpu.VMEM((heads, head_dim, PAGE_SIZE), F32)])
    o = pl.pallas_call(
        functools.partial(_moba_decode_kernel, past_len=n_pages * PAGE_SIZE),
        grid_spec=grid_spec,
        out_shape=jax.ShapeDtypeStruct((n, heads, head_dim, 1), F32),
        compiler_params=_params("arbitrary", "arbitrary"),
        name="moba_decode",
    )(page_table, q_b, kv_new[..., None], slopes, *([pages_t] * pages_per_step))
    return o.reshape(n, heads, head_dim)


def _moba_layer(yp, ys, cache, page_table, g_mix, w_qkv, q_gain, k_gain, w_o):
    b, s, d = yp.shape
    n = ys.shape[0]
    width = w_qkv.shape[1] // 3
    head_dim = width // MOBA_HEADS
    w16 = w_qkv.astype(BF16)
    wo16 = w_o.astype(BF16)
    gains = jnp.stack([_tiled_gain(q_gain, MOBA_HEADS), _tiled_gain(k_gain, MOBA_HEADS), jnp.ones((width,), F32)])
    gains = gains.reshape(3, 1, width)
    q32, kv32, k16, vt16, km = _moba_proj(yp, g_mix, w16, gains, prompt=True)
    yp_new = _moba_attn(q32, k16, vt16, km.reshape(b, s // MOBA_BLOCK, width), wo16, yp)
    rows_p = kv32.reshape(b, s, 2, MOBA_HEADS, head_dim)
    qs32, kvs32 = _moba_proj(ys[None], g_mix, w16, gains, prompt=False)
    att_s = _moba_decode(qs32.reshape(n, MOBA_HEADS, head_dim), kvs32.reshape(n, 2, MOBA_HEADS, head_dim),
                         cache, page_table)
    ys_new = _out_proj(att_s.reshape(n, width), wo16, ys)
    rows_s = kvs32.reshape(n, 1, 2, MOBA_HEADS, head_dim)
    return yp_new, ys_new, rows_p, rows_s


def kernel(x_prompt, x_sample, state_pool, cache_dil0, cache_dil1, cache_dil2, state_conv, cache_moba, page_table,
           norm_mix, norm_ffn, pool_w, pool_scale, dil_w_qkv, dil_q_gain, dil_k_gain, dil_w_o,
           moba_w_qkv, moba_q_gain, moba_k_gain, moba_w_o, ffn_w_in, ffn_conv_w, ffn_conv_b, ffn_w_out):
    assert x_sample.shape[1] == 1, "the sample group decodes one token per sequence"
    depth = norm_mix.shape[0]
    yp, ys = x_prompt, x_sample[:, 0]
    pool_p, pool_s, moba_p, moba_s, conv_p, conv_s = [], [], [], [], [], []
    dil_p = [[] for _ in DIL_PATTERNS]
    dil_s = [[] for _ in DIL_PATTERNS]
    for layer in range(depth):
        kind, j = layer % N_MIXERS, layer // N_MIXERS
        g_mix = norm_mix[layer][None]
        if kind == 0:
            w16 = pool_w[j].astype(BF16)
            scale = pool_scale[j][None]
            yp, tail = _pool_prompt(yp, g_mix, w16, scale)
            ys, hs = _pool_sample(ys, state_pool[j].transpose(1, 0, 2), g_mix, w16, scale)
            pool_p.append(tail[:, POOL_CARRY - POOL_HIST:])
            pool_s.append(jnp.concatenate([state_pool[j][:, 1:], hs[:, None]], axis=1))
        elif kind == 1:
            caches = [c[j] for c in (cache_dil0, cache_dil1, cache_dil2)]
            yp, ys, rows_p, rows_s = _dil_layer(yp, ys, caches, g_mix, dil_w_qkv[j], dil_q_gain[j], dil_k_gain[j],
                                                dil_w_o[j])
            for g in range(len(DIL_PATTERNS)):
                dil_p[g].append(rows_p[g])
                dil_s[g].append(rows_s[g])
        else:
            yp, ys, rows_p, rows_s = _moba_layer(yp, ys, cache_moba[j], page_table, g_mix, moba_w_qkv[j],
                                                 moba_q_gain[j], moba_k_gain[j], moba_w_o[j])
            moba_p.append(rows_p)
            moba_s.append(rows_s)
        g_ffn = norm_ffn[layer][None]
        win16 = ffn_w_in[layer].astype(BF16)
        wout16 = ffn_w_out[layer].astype(BF16)
        conv_w, conv_b = ffn_conv_w[layer], ffn_conv_b[layer][None]
        yp, tail = _ffn_prompt(yp, g_ffn, win16, conv_w, conv_b, wout16)
        ys, ua, ug = _ffn_sample(ys, g_ffn, win16, state_conv[layer].transpose(1, 0, 2), conv_w, conv_b, wout16)
        conv_p.append(tail[:, CONV_CARRY - CONV_HIST:])
        u_new = jnp.concatenate([ua, ug], axis=-1)[:, None]
        conv_s.append(jnp.concatenate([state_conv[layer][:, 1:], u_new], axis=1))
    stack = lambda xs: jnp.stack(xs, axis=0)
    return (yp, ys[:, None], stack(pool_p), stack(pool_s),
            stack(dil_p[0]), stack(dil_p[1]), stack(dil_p[2]),
            stack(dil_s[0]), stack(dil_s[1]), stack(dil_s[2]),
            stack(moba_p), stack(moba_s), stack(conv_p), stack(conv_s))
```

```python
import functools

import jax
import jax.numpy as jnp
from jax import lax
from jax.experimental import pallas as pl
from jax.experimental.pallas import tpu as pltpu

F32 = jnp.float32
BF16 = jnp.bfloat16

N_MIXERS = 3
POOL_WINDOWS = (2, 4, 8, 16)
POOL_HIST = max(POOL_WINDOWS) - 1
POOL_CARRY = 16
DIL_PATTERNS = ((128, 1), (512, 4), (2048, 16))
DIL_BAND = 128
DIL_HEADS = 8
MOBA_BLOCK = 256
MOBA_TOPK = 3
MOBA_HEADS = 16
HEAD_GROUP = 16
PAST_PER_STEP = 2
ONES_ROWS = 16
DECODE_BLOCKS = 4
PAGE_SIZE = 128
CONV_WIDTH = 3
CONV_HIST = CONV_WIDTH - 1
CONV_CARRY = 8
RMS_EPS = 1e-6
LOG2E = 1.4426950408889634
ALIBI_PIECES = 3
NEG_INF = float("-inf")
MASKED_MAX = 1e30
VMEM_LIMIT = 56 * 1024 * 1024


def _params(*sem):
    return pltpu.CompilerParams(dimension_semantics=sem, vmem_limit_bytes=VMEM_LIMIT)


def _resident(shape):
    zeros = (0,) * len(shape)
    return pl.BlockSpec(shape, lambda *_: zeros, pipeline_mode=pl.Buffered(1))


def _rms(x, g):
    return x * lax.rsqrt(jnp.mean(x * x, axis=-1, keepdims=True) + RMS_EPS) * g


def _dot(a, b):
    return jnp.dot(a, b, preferred_element_type=F32)


def _dot_nt(a, b, precision=None):
    return lax.dot_general(a, b, (((1,), (1,)), ((), ())), preferred_element_type=F32, precision=precision)


def _as_dot_operand(x):
    return x.astype(BF16).astype(F32)


def _dot_tn(a, b):
    return lax.dot_general(a, b, (((0,), (0,)), ((), ())), preferred_element_type=F32)


def _alibi_slopes(n):
    return [2.0 ** (-8.0 * i / n) for i in range(1, n + 1)]


def _pool_prompt_kernel(x_ref, g_ref, w_ref, sc_ref, y_ref, tail_ref, ext_ref, *, tile):
    t = pl.program_id(1)
    d_model = x_ref.shape[-1]
    gdim = d_model // len(POOL_WINDOWS)

    @pl.when(t == 0)
    def _():
        ext_ref[0:POOL_CARRY, :] = jnp.zeros((POOL_CARRY, d_model), F32)

    x = x_ref[0]
    ext_ref[POOL_CARRY:POOL_CARRY + tile, :] = _rms(x, g_ref[...])
    row = t * tile + lax.broadcasted_iota(jnp.int32, (tile, 1), 0)
    for gi, win in enumerate(POOL_WINDOWS):
        cols = slice(gi * gdim, (gi + 1) * gdim)
        cur = ext_ref[POOL_CARRY:POOL_CARRY + tile, cols]
        acc = cur
        for j in range(1, win):
            acc = acc + ext_ref[POOL_CARRY - j:POOL_CARRY - j + tile, cols]
        cnt = jnp.minimum(row + 1, win).astype(F32)
        d = acc / cnt - cur
        y = _dot(d.astype(BF16), w_ref[gi])
        y_ref[0, :, cols] = x[:, cols] + y * sc_ref[:, cols]
    last = ext_ref[tile:tile + POOL_CARRY, :]
    ext_ref[0:POOL_CARRY, :] = last

    @pl.when(t == pl.num_programs(1) - 1)
    def _():
        tail_ref[0] = last


def _pool_prompt(x, g, w16, sc, tile=512):
    b, s, d = x.shape
    tile = min(tile, s)
    return pl.pallas_call(
        functools.partial(_pool_prompt_kernel, tile=tile),
        grid=(b, s // tile),
        in_specs=[pl.BlockSpec((1, tile, d), lambda i, t: (i, t, 0)),
                  _resident((1, d)), _resident(w16.shape), _resident((1, d))],
        out_specs=[pl.BlockSpec((1, tile, d), lambda i, t: (i, t, 0)),
                   pl.BlockSpec((1, POOL_CARRY, d), lambda i, t: (i, 0, 0))],
        out_shape=[jax.ShapeDtypeStruct((b, s, d), F32), jax.ShapeDtypeStruct((b, POOL_CARRY, d), F32)],
        scratch_shapes=[pltpu.VMEM((POOL_CARRY + tile, d), F32)],
        compiler_params=_params("arbitrary", "arbitrary"),
        name="pool_prompt",
    )(x, g, w16, sc)


def _pool_sample_kernel(x_ref, st_ref, g_ref, w_ref, sc_ref, y_ref, h_ref):
    d_model = x_ref.shape[-1]
    gdim = d_model // len(POOL_WINDOWS)
    x = x_ref[...]
    h = _rms(x, g_ref[...])
    h_ref[...] = h
    for gi, win in enumerate(POOL_WINDOWS):
        cols = slice(gi * gdim, (gi + 1) * gdim)
        cur = h[:, cols]
        acc = cur
        for j in range(1, win):
            acc = acc + st_ref[POOL_HIST - j, :, cols]
        d = acc / float(win) - cur
        y = _dot(d.astype(BF16), w_ref[gi])
        y_ref[:, cols] = x[:, cols] + y * sc_ref[:, cols]


def _pool_sample(x, state_t, g, w16, sc):
    n, d = x.shape
    return pl.pallas_call(
        _pool_sample_kernel,
        out_shape=[jax.ShapeDtypeStruct((n, d), F32), jax.ShapeDtypeStruct((n, d), F32)],
        compiler_params=pltpu.CompilerParams(vmem_limit_bytes=VMEM_LIMIT),
        name="pool_sample",
    )(x, state_t, g, w16, sc)


def _silu(g):
    return g * (1.0 / (1.0 + jnp.exp(-g)))


def _ffn_prompt_kernel(y_ref, g_ref, win_ref, cw_ref, cb_ref, wout_ref, o_ref, tail_ref,
                       u_even_ref, u_odd_ref, carry_ref, acc_ref, *, tile, chunk):
    t = pl.program_id(1)
    hidden = wout_ref.shape[0]
    u_bufs = (u_even_ref, u_odd_ref)

    @pl.when(t == 0)
    def _():
        carry_ref[...] = jnp.zeros(carry_ref.shape, F32)

    y = y_ref[0]
    h = _rms(y, g_ref[...]).astype(BF16)
    n_chunks = hidden // chunk

    def up_project(c):
        us = []
        for part in range(2):
            cols = slice(part * hidden + c * chunk, part * hidden + (c + 1) * chunk)
            u = _dot(h, win_ref[:, cols])
            u_bufs[c % 2][part, 0:CONV_CARRY, :] = carry_ref[:, cols]
            u_bufs[c % 2][part, CONV_CARRY:CONV_CARRY + tile, :] = u
            carry_ref[:, cols] = u[tile - CONV_CARRY:tile, :]
            us.append(u)
        return us

    def conv_gate(c, us):
        conv = []
        for part in range(2):
            cols = slice(part * hidden + c * chunk, part * hidden + (c + 1) * chunk)
            cv = cb_ref[:, cols] + cw_ref[2:3, cols] * us[part]
            cv = cv + cw_ref[1:2, cols] * u_bufs[c % 2][part, CONV_CARRY - 1:CONV_CARRY - 1 + tile, :]
            cv = cv + cw_ref[0:1, cols] * u_bufs[c % 2][part, CONV_CARRY - 2:CONV_CARRY - 2 + tile, :]
            conv.append(cv)
        return (_silu(conv[1]) * conv[0]).astype(BF16)

    us_next = up_project(0)
    for c in range(n_chunks):
        us = us_next
        if c + 1 < n_chunks:
            us_next = up_project(c + 1)
        contrib = _dot(conv_gate(c, us), wout_ref[c * chunk:(c + 1) * chunk, :])
        if c == 0:
            acc_ref[...] = contrib
        else:
            acc_ref[...] += contrib
    o_ref[0] = y + acc_ref[...]

    @pl.when(t == pl.num_programs(1) - 1)
    def _():
        tail_ref[0] = carry_ref[...]


def _ffn_prompt(y, g, win16, cw, cb, wout16, tile=512, chunk=256):
    b, s, d = y.shape
    hidden = wout16.shape[0]
    tile = min(tile, s)
    return pl.pallas_call(
        functools.partial(_ffn_prompt_kernel, tile=tile, chunk=chunk),
        grid=(b, s // tile),
        in_specs=[pl.BlockSpec((1, tile, d), lambda i, t: (i, t, 0)),
                  _resident((1, d)), _resident(win16.shape), _resident(cw.shape), _resident(cb.shape),
                  _resident(wout16.shape)],
        out_specs=[pl.BlockSpec((1, tile, d), lambda i, t: (i, t, 0)),
                   pl.BlockSpec((1, CONV_CARRY, 2 * hidden), lambda i, t: (i, 0, 0))],
        out_shape=[jax.ShapeDtypeStruct((b, s, d), F32),
                   jax.ShapeDtypeStruct((b, CONV_CARRY, 2 * hidden), F32)],
        scratch_shapes=[pltpu.VMEM((2, CONV_CARRY + tile, chunk), F32),
                        pltpu.VMEM((2, CONV_CARRY + tile, chunk), F32),
                        pltpu.VMEM((CONV_CARRY, 2 * hidden), F32),
                        pltpu.VMEM((tile, d), F32)],
        compiler_params=_params("arbitrary", "arbitrary"),
        name="ffn_prompt",
    )(y, g, win16, cw, cb, wout16)


def _ffn_sample_kernel(y_ref, g_ref, wa_ref, wg_ref, ha_ref, hg_ref, cwa_ref, cwg_ref, cba_ref, cbg_ref,
                       wout_ref, o_ref, ua_ref, ug_ref):
    c = pl.program_id(0)
    y = y_ref[...]
    h = _rms(y, g_ref[...]).astype(BF16)

    def conv(w_ref, hist_ref, cw_ref, cb_ref, u_out_ref):
        u = _dot(h, w_ref[...])
        u_out_ref[...] = u
        return cb_ref[...] + cw_ref[0:1, :] * hist_ref[0] + cw_ref[1:2, :] * hist_ref[1] + cw_ref[2:3, :] * u

    ca = conv(wa_ref, ha_ref, cwa_ref, cba_ref, ua_ref)
    cg = conv(wg_ref, hg_ref, cwg_ref, cbg_ref, ug_ref)
    contrib = _dot((_silu(cg) * ca).astype(BF16), wout_ref[...])

    @pl.when(c == 0)
    def _():
        o_ref[...] = y + contrib

    @pl.when(c != 0)
    def _():
        o_ref[...] += contrib


def _ffn_sample(y, g, win16, hist_t, cw, cb, wout16, chunk=256):
    n, d = y.shape
    hidden = wout16.shape[0]
    nc = hidden // chunk
    a_col = lambda c: (0, c)
    g_col = lambda c: (0, c + nc)
    return pl.pallas_call(
        _ffn_sample_kernel,
        grid=(nc,),
        in_specs=[_resident((n, d)), _resident((1, d)),
                  pl.BlockSpec((d, chunk), a_col), pl.BlockSpec((d, chunk), g_col),
                  pl.BlockSpec((CONV_HIST, n, chunk), lambda c: (0, 0, c)),
                  pl.BlockSpec((CONV_HIST, n, chunk), lambda c: (0, 0, c + nc)),
                  pl.BlockSpec((CONV_WIDTH, chunk), a_col), pl.BlockSpec((CONV_WIDTH, chunk), g_col),
                  pl.BlockSpec((1, chunk), a_col), pl.BlockSpec((1, chunk), g_col),
                  pl.BlockSpec((chunk, d), lambda c: (c, 0))],
        out_specs=[pl.BlockSpec((n, d), lambda c: (0, 0)),
                   pl.BlockSpec((n, chunk), a_col), pl.BlockSpec((n, chunk), a_col)],
        out_shape=[jax.ShapeDtypeStruct((n, d), F32), jax.ShapeDtypeStruct((n, hidden), F32),
                   jax.ShapeDtypeStruct((n, hidden), F32)],
        compiler_params=_params("arbitrary"),
        name="ffn_sample",
    )(y, g, win16, win16, hist_t, hist_t, cw, cw, cb, cb, wout16)


def _out_proj_kernel(a_ref, w_ref, x_ref, y_ref):
    y_ref[...] = x_ref[...] + _dot(a_ref[...].astype(BF16), w_ref[...])


def _out_proj(a, w16, x, tile=512):
    n, d = x.shape
    tile = min(tile, n)
    return pl.pallas_call(
        _out_proj_kernel,
        grid=(n // tile,),
        in_specs=[pl.BlockSpec((tile, a.shape[1]), lambda t: (t, 0)), _resident(w16.shape),
                  pl.BlockSpec((tile, d), lambda t: (t, 0))],
        out_specs=pl.BlockSpec((tile, d), lambda t: (t, 0)),
        out_shape=jax.ShapeDtypeStruct((n, d), F32),
        compiler_params=_params("arbitrary"),
        name="out_proj",
    )(a, w16, x)


def _out_proj_heads_kernel(a_ref, w_ref, x_ref, y_ref):
    a = jnp.concatenate([a_ref[0, h] for h in range(a_ref.shape[1])], axis=1)
    y_ref[0] = x_ref[0] + _dot(a.astype(BF16), w_ref[...])


def _out_proj_heads(a, w16, x, tile=512):
    b, heads, s, head_dim = a.shape
    d = x.shape[-1]
    tile = min(tile, s)
    return pl.pallas_call(
        _out_proj_heads_kernel,
        grid=(b, s // tile),
        in_specs=[pl.BlockSpec((1, heads, tile, head_dim), lambda i, t: (i, 0, t, 0)), _resident(w16.shape),
                  pl.BlockSpec((1, tile, d), lambda i, t: (i, t, 0))],
        out_specs=pl.BlockSpec((1, tile, d), lambda i, t: (i, t, 0)),
        out_shape=jax.ShapeDtypeStruct((b, s, d), F32),
        compiler_params=_params("arbitrary", "arbitrary"),
        name="out_proj_heads",
    )(a, w16, x)


def _head_norm(r, gain, head_dim):
    lanes = 128
    low = lax.broadcasted_iota(jnp.int32, (1, lanes), 1) < head_dim
    cols = []
    for v in range(r.shape[1] // lanes):
        seg = r[:, v * lanes:(v + 1) * lanes]
        sq = seg * seg
        if head_dim == lanes:
            ms = jnp.mean(sq, axis=-1, keepdims=True)
        else:
            lo = jnp.sum(jnp.where(low, sq, 0.0), axis=-1, keepdims=True)
            hi = jnp.sum(jnp.where(low, 0.0, sq), axis=-1, keepdims=True)
            ms = jnp.where(low, lo, hi) * (1.0 / head_dim)
        cols.append(seg * lax.rsqrt(ms + RMS_EPS))
    return jnp.concatenate(cols, axis=1) * gain


def _tiled_gain(gain, heads):
    return jnp.tile(gain.astype(F32), heads)


def _dil_proj_kernel(x_ref, g_ref, w_ref, gain_ref, o0_ref, o1_ref, o2_ref, kv32_ref, h_ref, r_ref,
                     *, dils, head_dim):
    j = pl.program_id(2)
    g = j // 3
    c = j % 3
    chunks, tile, lanes = r_ref.shape

    @pl.when(j == 0)
    def _():
        h_ref[...] = _rms(x_ref[0], g_ref[...]).astype(BF16)

    r = _dot(h_ref[...], w_ref[...])

    def emit(val):
        @pl.when(c > 0)
        def _():
            kv32_ref[0] = val

        for gi, (o_ref, dil) in enumerate(zip((o0_ref, o1_ref, o2_ref), dils)):
            @pl.when(g == gi)
            def _(o_ref=o_ref, dil=dil):
                if dil == 1:
                    o_ref[0, 0] = val.astype(BF16)
                    return
                for ch in range(chunks):
                    r_ref[ch] = val[:, ch * lanes:(ch + 1) * lanes]
                for res in range(dil):
                    for ch in range(chunks):
                        o_ref[0, res, :, ch * lanes:(ch + 1) * lanes] = (
                            r_ref[ch, pl.ds(res, tile // dil, stride=dil), :].astype(BF16))

    @pl.when(c < 2)
    def _():
        emit(_head_norm(r, gain_ref[0], head_dim))

    @pl.when(c == 2)
    def _():
        emit(r)


def _dil_proj(x, g, w16, gains, tail_rows, dils, tile=1024):
    b, s, d = x.shape
    groups = len(dils)
    width = w16.shape[1] // (3 * groups)
    tile = min(tile, s, tail_rows)
    t0 = (s - tail_rows) // tile

    def kv_map(i, t, j):
        col = (j // 3) * 2 + jnp.maximum(j % 3 - 1, 0)
        return i, jnp.maximum(t - t0, 0), jnp.where(t >= t0, col, 0)

    def group_spec(gi, dil):
        return pl.BlockSpec((1, dil, tile // dil, width),
                            lambda i, t, j: (i, 0, t, jnp.clip(j - 3 * gi, 0, 2)))

    return pl.pallas_call(
        functools.partial(_dil_proj_kernel, dils=tuple(dils), head_dim=width // DIL_HEADS),
        grid=(b, s // tile, 3 * groups),
        in_specs=[pl.BlockSpec((1, tile, d), lambda i, t, j: (i, t, 0)), _resident((1, d)),
                  pl.BlockSpec((d, width), lambda i, t, j: (0, j)),
                  pl.BlockSpec((1, 1, width), lambda i, t, j: (j, 0, 0))],
        out_specs=[group_spec(gi, dil) for gi, dil in enumerate(dils)]
        + [pl.BlockSpec((1, tile, width), kv_map)],
        out_shape=[jax.ShapeDtypeStruct((b, dil, s // dil, 3 * width), BF16) for dil in dils]
        + [jax.ShapeDtypeStruct((b, tail_rows, groups * 2 * width), F32)],
        scratch_shapes=[pltpu.VMEM((tile, d), BF16), pltpu.VMEM((width // 128, tile, 128), F32)],
        compiler_params=_params("arbitrary", "arbitrary", "arbitrary"),
        name="dil_proj",
    )(x, g, w16, gains)


def _moba_proj_kernel(x_ref, g_ref, w_ref, gain_ref, *refs, head_dim, prompt):
    if prompt:
        wvt_ref, q32_ref, kv32_ref, k16_ref, vt16_ref, km_ref, h_ref = refs
    else:
        q32_ref, kv32_ref, h_ref = refs
    j = pl.program_id(2)

    @pl.when(j == 0)
    def _():
        h_ref[...] = _rms(x_ref[0], g_ref[...]).astype(BF16)

    r = _dot(h_ref[...], w_ref[...])

    @pl.when(j == 0)
    def _():
        q32_ref[0] = _head_norm(r, gain_ref[0], head_dim)

    @pl.when(j == 1)
    def _():
        rn = _head_norm(r, gain_ref[0], head_dim)
        kv32_ref[0] = rn
        if prompt:
            lanes = 2 * head_dim
            lane = lax.broadcasted_iota(jnp.int32, (1, lanes), 1)
            in_block = (lax.broadcasted_iota(jnp.int32, (rn.shape[0], 1), 0) % MOBA_BLOCK).astype(F32)
            for h, slope in enumerate(_alibi_slopes(rn.shape[1] // head_dim)):
                kp = rn[:, (h // 2) * lanes:(h // 2 + 1) * lanes]
                own = (lane < head_dim) if h % 2 == 0 else (lane >= head_dim)
                base = head_dim if h % 2 == 0 else 0
                rest = in_block * (slope * LOG2E)
                term = jnp.zeros((1, lanes), F32)
                for piece in range(ALIBI_PIECES):
                    part = rest if piece == ALIBI_PIECES - 1 else rest.astype(BF16).astype(F32)
                    term = jnp.where(lane == base + piece, part, term)
                    rest = rest - part
                k16_ref[0, :, h * lanes:(h + 1) * lanes] = jnp.where(own, kp, term).astype(BF16)
            for blk in range(rn.shape[0] // MOBA_BLOCK):
                rows = rn[blk * MOBA_BLOCK:(blk + 1) * MOBA_BLOCK, :]
                km_ref[0, blk] = jnp.mean(rows, axis=0, keepdims=True)

    @pl.when(j == 2)
    def _():
        kv32_ref[0] = r
        if prompt:
            vt16_ref[0] = _dot_nt(wvt_ref[...], h_ref[...]).astype(BF16)


def _moba_proj(x, g, w16, gains, prompt, tile=1024):
    b, s, d = x.shape
    width = w16.shape[1] // 3
    tile = min(tile, s)
    in_specs = [pl.BlockSpec((1, tile, d), lambda i, t, j: (i, t, 0)), _resident((1, d)),
                pl.BlockSpec((d, width), lambda i, t, j: (0, j)),
                pl.BlockSpec((1, 1, width), lambda i, t, j: (j, 0, 0))]
    args = [x, g, w16, gains]
    out_specs = [pl.BlockSpec((1, tile, width), lambda i, t, j: (i, t, 0)),
                 pl.BlockSpec((1, tile, width), lambda i, t, j: (i, t, jnp.maximum(j - 1, 0)))]
    out_shape = [jax.ShapeDtypeStruct((b, s, width), F32), jax.ShapeDtypeStruct((b, s, 2 * width), F32)]
    if prompt:
        in_specs.append(_resident((width, d)))
        args.append(w16[:, 2 * width:].T)
        out_specs += [pl.BlockSpec((1, tile, 2 * width), lambda i, t, j: (i, t, 0)),
                      pl.BlockSpec((1, width, tile), lambda i, t, j: (i, 0, t)),
                      pl.BlockSpec((1, tile // MOBA_BLOCK, 1, width), lambda i, t, j: (i, t, 0, 0))]
        out_shape += [jax.ShapeDtypeStruct((b, s, 2 * width), BF16), jax.ShapeDtypeStruct((b, width, s), BF16),
                      jax.ShapeDtypeStruct((b, s // MOBA_BLOCK, 1, width), F32)]
    return pl.pallas_call(
        functools.partial(_moba_proj_kernel, head_dim=width // MOBA_HEADS, prompt=prompt),
        grid=(b, s // tile, 3),
        in_specs=in_specs,
        out_specs=out_specs,
        out_shape=out_shape,
        scratch_shapes=[pltpu.VMEM((tile, d), BF16)],
        compiler_params=_params("arbitrary", "arbitrary", "arbitrary"),
        name="moba_proj",
    )(*args)


def _dil_attn_kernel(q_ref, kp_ref, ko_ref, vp_ref, vo_ref, *refs, dil, slopes, head_dim, chained, last):
    refs = list(refs)
    oin_ref, lin_ref = (refs.pop(0), refs.pop(0)) if chained else (None, None)
    o_ref = refs.pop(0)
    l_ref = None if last else refs.pop(0)
    n = pl.program_id(1)
    r = pl.program_id(2)
    band = DIL_BAND
    width = len(slopes) * head_dim
    scale = head_dim ** -0.5
    rows = slice(None) if dil == 1 else pl.ds(r, band, stride=dil)
    qi = lax.broadcasted_iota(jnp.int32, (band, band), 0)
    kj = lax.broadcasted_iota(jnp.int32, (band, band), 1)
    nd_own = jnp.where(kj <= qi, ((kj - qi) * dil).astype(F32), NEG_INF)
    nd_prev = jnp.where((kj >= qi) & (n > 0), ((kj - qi - band) * dil).astype(F32), NEG_INF)
    heads = range(len(slopes))
    head_cols = [slice(h * head_dim, (h + 1) * head_dim) for h in heads]
    scores = []
    for h in heads:
        q = q_ref[0, 0, :, head_cols[h]]
        scores.append((_dot_nt(q, ko_ref[0, 0, :, head_cols[h]]) * scale + slopes[h] * nd_own,
                       _dot_nt(q, kp_ref[0, 0, :, head_cols[h]]) * scale + slopes[h] * nd_prev))
    probs = []
    for h, (s_o, s_p) in zip(heads, scores):
        m = jnp.maximum(jnp.max(s_o, axis=-1, keepdims=True), jnp.max(s_p, axis=-1, keepdims=True))
        if chained:
            lse_prev = lin_ref[0, h, rows, :]
            m = jnp.maximum(m, lse_prev)
        w_prev = jnp.exp(lse_prev - m) if chained else None
        probs.append((jnp.exp(s_o - m).astype(BF16), jnp.exp(s_p - m).astype(BF16), m, w_prev))
    ones = jnp.ones((band, head_dim), BF16)
    for h, (p_o, p_p, m, w_prev) in zip(heads, probs):
        pv = (_dot(p_o, jnp.concatenate([vo_ref[0, 0, :, head_cols[h]], ones], axis=1))
              + _dot(p_p, jnp.concatenate([vp_ref[0, 0, :, head_cols[h]], ones], axis=1)))
        acc, l = pv[:, 0:head_dim], pv[:, head_dim:2 * head_dim]
        if chained:
            acc = acc + w_prev * oin_ref[0, h, rows, :]
            l = l + w_prev
        o_ref[0, h, rows, :] = acc / l
        if not last:
            l_ref[0, h, rows, :] = m + jnp.log(l)


def _dil_attn(qkv, g, state):
    b, dil, n_sub, ncol = qkv.shape
    width = ncol // 3
    head_dim = width // DIL_HEADS
    s = n_sub * dil
    n_blk = n_sub // DIL_BAND
    groups = len(DIL_PATTERNS)
    last = g == groups - 1
    slopes = _alibi_slopes(groups * DIL_HEADS)[g * DIL_HEADS:(g + 1) * DIL_HEADS]

    def spec(c, prev):
        return pl.BlockSpec((1, 1, DIL_BAND, width),
                            lambda i, n, r: (i, r, (jnp.maximum(n - 1, 0) if prev else n), c))

    rows = DIL_BAND * dil
    o_spec = pl.BlockSpec((1, DIL_HEADS, rows, head_dim), lambda i, n, r: (i, 0, n, 0))
    in_specs = [spec(0, False), spec(1, True), spec(1, False), spec(2, True), spec(2, False)]
    args = [qkv] * 5
    if state is not None:
        in_specs += [o_spec, o_spec]
        args += list(state)
    out_specs = [o_spec] if last else [o_spec, o_spec]
    out_shape = [jax.ShapeDtypeStruct((b, DIL_HEADS, s, head_dim), F32)] * (1 if last else 2)
    return pl.pallas_call(
        functools.partial(_dil_attn_kernel, dil=dil, slopes=slopes, head_dim=head_dim,
                          chained=state is not None, last=last),
        grid=(b, n_blk, dil),
        in_specs=in_specs,
        out_specs=out_specs,
        out_shape=out_shape,
        compiler_params=_params("arbitrary", "arbitrary", "arbitrary"),
        name=f"dil_attn{g}",
    )(*args)


def _dil_decode_kernel(q_ref, kvn_ref, slope_ref, c0_ref, c1_ref, c2_ref, o_ref, *, head_dim):
    scale = head_dim ** -0.5
    steps = (DIL_BAND - lax.broadcasted_iota(jnp.int32, (DIL_BAND, 1, 1), 0)).astype(F32)
    outs, lses = [], []
    for g, c_ref in enumerate((c0_ref, c1_ref, c2_ref)):
        dil = DIL_PATTERNS[g][1]
        q = q_ref[0, g]
        kn = _as_dot_operand(kvn_ref[0, 2 * g])
        vn = _as_dot_operand(kvn_ref[0, 2 * g + 1])
        kc = _as_dot_operand(c_ref[0, :, 0, 0])
        vc = _as_dot_operand(c_ref[0, :, 0, 1])
        slope = slope_ref[g][:, 0:1]
        s = jnp.sum(kc * q[None], axis=-1, keepdims=True) * scale - slope[None] * (steps * float(dil))
        s_new = jnp.sum(kn * q, axis=-1, keepdims=True) * scale
        m = jnp.maximum(jnp.max(s, axis=0), s_new)
        p = jnp.exp(s - m[None])
        p_new = jnp.exp(s_new - m)
        l = jnp.sum(p, axis=0) + p_new
        acc = jnp.sum(_as_dot_operand(p) * vc, axis=0) + _as_dot_operand(p_new) * vn
        outs.append(acc / l)
        lses.append(m + jnp.log(l))
    m = jnp.maximum(jnp.maximum(lses[0], lses[1]), lses[2])
    es = [jnp.exp(l - m) for l in lses]
    o_ref[0] = (es[0] * outs[0] + es[1] * outs[1] + es[2] * outs[2]) / (es[0] + es[1] + es[2])


def _dil_decode(q, kv_new, caches):
    n, _, heads, head_dim = q.shape
    views, specs = [], []
    for g, (win, dil) in enumerate(DIL_PATTERNS):
        assert caches[g].shape[1] == win, "the window buffers must be full"
        views.append(caches[g].reshape(n, DIL_BAND, dil, 2, heads, head_dim))
        specs.append(pl.BlockSpec((1, DIL_BAND, 1, 2, heads, head_dim), lambda i: (i, 0, 0, 0, 0, 0)))
    slopes = jnp.asarray(_alibi_slopes(len(DIL_PATTERNS) * heads), F32).reshape(len(DIL_PATTERNS), heads, 1)
    slopes = jnp.broadcast_to(slopes, (len(DIL_PATTERNS), heads, head_dim))
    return pl.pallas_call(
        functools.partial(_dil_decode_kernel, head_dim=head_dim),
        grid=(n,),
        in_specs=[pl.BlockSpec((1,) + q.shape[1:], lambda i: (i, 0, 0, 0)),
                  pl.BlockSpec((1,) + kv_new.shape[1:], lambda i: (i, 0, 0, 0)),
                  _resident(slopes.shape)] + specs,
        out_specs=pl.BlockSpec((1, heads, head_dim), lambda i: (i, 0, 0)),
        out_shape=jax.ShapeDtypeStruct((n, heads, head_dim), F32),
        compiler_params=_params("arbitrary"),
        name="dil_decode",
    )(q, kv_new, slopes, *views)


def _dil_layer(yp, ys, caches, g_mix, w_qkv, q_gain, k_gain, w_o):
    b, s, d = yp.shape
    n = ys.shape[0]
    groups = len(DIL_PATTERNS)
    width = w_qkv.shape[1] // (3 * groups)
    head_dim = width // DIL_HEADS
    w16 = w_qkv.astype(BF16)
    wo16 = w_o.astype(BF16)
    ones = jnp.ones((width,), F32)
    gains = jnp.stack([row for g in range(groups)
                       for row in (_tiled_gain(q_gain[g], DIL_HEADS), _tiled_gain(k_gain[g], DIL_HEADS), ones)])
    gains = gains.reshape(3 * groups, 1, width)
    tail = min(max(w for w, _ in DIL_PATTERNS), s)
    *qkv, kv32 = _dil_proj(yp, g_mix, w16, gains, tail, [dil for _, dil in DIL_PATTERNS])
    state = None
    for g in range(groups):
        state = _dil_attn(qkv[g], g, state)
    yp_new = _out_proj_heads(state[0], wo16, yp)
    rows_p = [kv32[:, tail - min(w, s):, g * 2 * width:(g + 1) * 2 * width].reshape(b, min(w, s), 2, DIL_HEADS, head_dim)
              for g, (w, _) in enumerate(DIL_PATTERNS)]
    *qkv_s, kvs32 = _dil_proj(ys[None], g_mix, w16, gains, n, [1] * groups)
    q_s = jnp.stack([a[0, 0, :, :width] for a in qkv_s], axis=1).astype(F32).reshape(n, groups, DIL_HEADS, head_dim)
    att = _dil_decode(q_s, kvs32.reshape(n, 2 * groups, DIL_HEADS, head_dim), caches)
    ys_new = _out_proj(att.reshape(n, width), wo16, ys)
    kvs = kvs32.reshape(n, 1, groups, 2, DIL_HEADS, head_dim)
    rows_s = [kvs[:, :, g] for g in range(groups)]
    return yp_new, ys_new, rows_p, rows_s


def _top_blocks(gate, block_idx, axis):
    n = gate.shape[axis]
    chosen = jnp.zeros(gate.shape, jnp.bool_)
    for _ in range(MOBA_TOPK):
        mx = jnp.max(gate, axis=axis, keepdims=True)
        idx = jnp.min(jnp.where(gate == mx, block_idx, n), axis=axis, keepdims=True)
        hit = (block_idx == idx) & (mx > NEG_INF)
        chosen = chosen | hit
        gate = jnp.where(block_idx == idx, NEG_INF, gate)
    return chosen


def _moba_attn_kernel(ti_ref, tj_ref, q_ref, kd_ref, vtd_ref, k_ref, vt_ref, km_ref, wo_ref, x_ref, y_ref,
                      qm_ref, bits_ref, m_ref, l_ref, acc_ref, *, slopes, head_dim):
    step = pl.program_id(1)
    i = ti_ref[step]
    j = tj_ref[step]
    blk = MOBA_BLOCK
    lanes = 2 * head_dim
    heads = len(slopes)
    scale = head_dim ** -0.5
    lane = lax.broadcasted_iota(jnp.int32, (1, lanes), 1)
    rk = lax.broadcasted_iota(jnp.int32, (blk, blk), 0)
    rq = lax.broadcasted_iota(jnp.int32, (blk, blk), 1)

    @pl.when(j < 0)
    def _():
        nb = km_ref.shape[1]
        bidx = lax.broadcasted_iota(jnp.int32, (nb, blk), 0)
        for pair in range(heads // 2):
            cols = slice(pair * lanes, (pair + 1) * lanes)
            qp = q_ref[0, :, cols]
            kmp = km_ref[0, :, cols]
            for half in range(2):
                h = 2 * pair + half
                own = (lane < head_dim) if half == 0 else (lane >= head_dim)
                base = head_dim if half == 0 else 0
                ones = jnp.where((lane >= base) & (lane < base + ALIBI_PIECES), 1.0, 0.0)
                qm_ref[h] = jnp.where(own, qp * (scale * LOG2E), ones).astype(BF16)
                gate = _dot_nt(kmp.astype(BF16), jnp.where(own, qp, 0.0).astype(BF16))
                chosen = _top_blocks(jnp.where(bidx < i, gate, NEG_INF), bidx, 0)
                bits_ref[h] = jnp.sum(jnp.where(chosen, jnp.left_shift(1, bidx), 0), axis=0, keepdims=True)

    def own_block_step(h, s):
        s = jnp.where(rk <= rq, s, NEG_INF)
        m_new = jnp.max(s, axis=0, keepdims=True)
        m_ref[h] = m_new
        return jnp.exp2(s - m_new).astype(BF16), None

    def past_blocks_step(h, parts):
        m_prev = m_ref[h]
        offsets, picked = [], []
        m_new = m_prev
        for sub, s in enumerate(parts):
            block = PAST_PER_STEP * j + sub
            offsets.append((slopes[h] * LOG2E) * ((block - i) * blk).astype(F32))
            picked.append((jnp.right_shift(bits_ref[h], block) & 1) == 1)
            m_new = jnp.maximum(m_new, jnp.where(picked[sub], jnp.max(s, axis=0, keepdims=True) + offsets[sub],
                                                 NEG_INF))
        alpha = jnp.exp2(m_prev - m_new)
        ps = [jnp.exp2(s - jnp.where(picked[sub], m_new - offsets[sub], MASKED_MAX)).astype(BF16)
              for sub, s in enumerate(parts)]
        m_ref[h] = m_new
        return jnp.concatenate(ps, axis=0), alpha

    def attend(first):
        ones_d = jnp.ones((ONES_ROWS, blk), BF16)
        ones_p = jnp.ones((ONES_ROWS, PAST_PER_STEP * blk), BF16)
        for h0 in range(0, heads, HEAD_GROUP):
            group = range(h0, h0 + HEAD_GROUP)
            if first:
                probs = [own_block_step(h, _dot_nt(kd_ref[0, :, h * lanes:(h + 1) * lanes], qm_ref[h]))
                         for h in group]
            else:
                scores = [[_dot_nt(k_ref[0, sub * blk:(sub + 1) * blk, h * lanes:(h + 1) * lanes], qm_ref[h])
                           for sub in range(PAST_PER_STEP)] for h in group]
                probs = [past_blocks_step(h, parts) for h, parts in zip(group, scores)]
            for h, (p, alpha) in zip(group, probs):
                rows = slice(h * head_dim, (h + 1) * head_dim)
                if first:
                    pv = _dot(jnp.concatenate([vtd_ref[0, rows, :], ones_d], axis=0), p)
                    acc_ref[rows, :] = pv[0:head_dim]
                    l_ref[h] = pv[head_dim:head_dim + 1]
                else:
                    pv = _dot(jnp.concatenate([vt_ref[0, rows, :], ones_p], axis=0), p)
                    acc_ref[rows, :] = alpha * acc_ref[rows, :] + pv[0:head_dim]
                    l_ref[h] = alpha * l_ref[h] + pv[head_dim:head_dim + 1]

    @pl.when(j < 0)
    def _():
        attend(True)

    @pl.when(j >= 0)
    def _():
        attend(False)

    @pl.when((i == 0) | (j == (i + PAST_PER_STEP - 1) // PAST_PER_STEP - 1))
    def _():
        parts = [acc_ref[h * head_dim:(h + 1) * head_dim, :] / l_ref[h] for h in range(heads)]
        out_t = jnp.concatenate(parts, axis=0).astype(BF16)
        y_ref[0] = x_ref[0] + _dot_tn(out_t, wo_ref[...])


def _moba_attn(q32, k16, vt16, km, wo16, x):
    b, s, width = q32.shape
    d = x.shape[-1]
    head_dim = width // MOBA_HEADS
    nt = s // MOBA_BLOCK
    assert nt <= 32, "the chosen (strictly earlier) blocks are kept as bits 0..30 of an int32"
    assert nt % PAST_PER_STEP == 0
    steps = [(i, j) for i in range(nt) for j in [-1, *range(-(-i // PAST_PER_STEP))]]
    ti = [i for i, _ in steps]
    tj = [j for _, j in steps]
    past = PAST_PER_STEP * MOBA_BLOCK
    q_tile = lambda n, t, ti, tj: (n, ti[t], 0)
    grid_spec = pltpu.PrefetchScalarGridSpec(
        num_scalar_prefetch=2,
        grid=(b, len(steps)),
        in_specs=[pl.BlockSpec((1, MOBA_BLOCK, width), q_tile),
                  pl.BlockSpec((1, MOBA_BLOCK, 2 * width), q_tile),
                  pl.BlockSpec((1, width, MOBA_BLOCK), lambda n, t, ti, tj: (n, 0, ti[t])),
                  pl.BlockSpec((1, past, 2 * width), lambda n, t, ti, tj: (n, jnp.maximum(tj[t], 0), 0)),
                  pl.BlockSpec((1, width, past), lambda n, t, ti, tj: (n, 0, jnp.maximum(tj[t], 0))),
                  pl.BlockSpec((1, nt, width), lambda n, t, ti, tj: (n, 0, 0)),
                  pl.BlockSpec(wo16.shape, lambda n, t, ti, tj: (0, 0), pipeline_mode=pl.Buffered(1)),
                  pl.BlockSpec((1, MOBA_BLOCK, d), q_tile)],
        out_specs=pl.BlockSpec((1, MOBA_BLOCK, d), q_tile),
        scratch_shapes=[pltpu.VMEM((MOBA_HEADS, MOBA_BLOCK, 2 * head_dim), BF16),
                        pltpu.VMEM((MOBA_HEADS, 1, MOBA_BLOCK), jnp.int32),
                        pltpu.VMEM((MOBA_HEADS, 1, MOBA_BLOCK), F32),
                        pltpu.VMEM((MOBA_HEADS, 1, MOBA_BLOCK), F32),
                        pltpu.VMEM((width, MOBA_BLOCK), F32)])
    return pl.pallas_call(
        functools.partial(_moba_attn_kernel, slopes=_alibi_slopes(MOBA_HEADS), head_dim=head_dim),
        grid_spec=grid_spec,
        out_shape=jax.ShapeDtypeStruct((b, s, d), F32),
        compiler_params=_params("arbitrary", "arbitrary"),
        name="moba_attn",
    )(jnp.asarray(ti, jnp.int32), jnp.asarray(tj, jnp.int32), q32, k16, vt16, k16, vt16, km, wo16, x)


def _moba_decode_kernel(pt_ref, q_ref, kvn_ref, slope_ref, *refs, past_len):
    page_refs = refs[:2 * DECODE_BLOCKS]
    o_ref, gate_ref, m_ref, l_ref, acc_ref = refs[2 * DECODE_BLOCKS:]
    j = pl.program_id(1)
    heads, head_dim, keys = q_ref.shape[1:]
    scale = head_dim ** -0.5
    q = q_ref[0]
    slope = slope_ref[...]
    lane = lax.broadcasted_iota(jnp.int32, (1, 1, keys), 2)

    @pl.when(j == 0)
    def _():
        gate_ref[...] = jnp.full(gate_ref.shape, NEG_INF, F32)
        m_ref[...] = jnp.zeros(m_ref.shape, F32)
        l_ref[...] = jnp.zeros(l_ref.shape, F32)
        acc_ref[...] = jnp.zeros(acc_ref.shape, F32)

    for sub in range(DECODE_BLOCKS):
        block = j * DECODE_BLOCKS + sub
        p0_ref, p1_ref = page_refs[2 * sub:2 * sub + 2]
        scores = []
        for half, page_ref in enumerate((p0_ref, p1_ref)):
            dist = (past_len - block * MOBA_BLOCK - half * keys - lane).astype(F32)
            qk = jnp.sum(q * page_ref[0, 0], axis=1, keepdims=True)
            scores.append(qk * scale - slope * dist)
        m = jnp.maximum(jnp.max(scores[0], axis=-1, keepdims=True), jnp.max(scores[1], axis=-1, keepdims=True))
        p0 = jnp.exp(scores[0] - m)
        p1 = jnp.exp(scores[1] - m)
        here = lane == block
        m_ref[...] = jnp.where(here, m, m_ref[...])
        l_ref[...] = jnp.where(here, jnp.sum(p0 + p1, axis=-1, keepdims=True), l_ref[...])
        acc = jnp.sum(p0 * p0_ref[0, 1] + p1 * p1_ref[0, 1], axis=-1, keepdims=True)
        acc_ref[...] = jnp.where(here, acc, acc_ref[...])
        k_mean = jnp.sum(p0_ref[0, 0] + p1_ref[0, 0], axis=-1, keepdims=True) * (1.0 / MOBA_BLOCK)
        gate = jnp.sum(q[:, :, 0:1] * _as_dot_operand(k_mean), axis=1, keepdims=True)
        gate_ref[...] = jnp.where(here, gate, gate_ref[...])

    @pl.when(j == pl.num_programs(1) - 1)
    def _():
        chosen = _top_blocks(gate_ref[...], jnp.broadcast_to(lane, gate_ref.shape), 2)
        kn = _as_dot_operand(kvn_ref[0, 0])
        vn = _as_dot_operand(kvn_ref[0, 1])
        s_new = jnp.sum(q * kn, axis=1, keepdims=True)[:, :, 0:1] * scale
        m_all = m_ref[...]
        m_tot = jnp.maximum(jnp.max(jnp.where(chosen, m_all, NEG_INF), axis=-1, keepdims=True), s_new)
        w = jnp.where(chosen, jnp.exp(m_all - m_tot), 0.0)
        e_new = jnp.exp(s_new - m_tot)
        den = jnp.sum(w * l_ref[...], axis=-1, keepdims=True) + e_new
        o_ref[0] = (jnp.sum(w * acc_ref[...], axis=-1, keepdims=True) + _as_dot_operand(e_new) * vn) / den


def _moba_decode(q, kv_new, cache, page_table):
    n, heads, head_dim = q.shape
    n_pages = page_table.shape[1]
    per_blk = MOBA_BLOCK // PAGE_SIZE
    assert cache.shape[1] == PAGE_SIZE and per_blk == 2 and n_pages % (per_blk * DECODE_BLOCKS) == 0
    nb = n_pages // per_blk
    assert nb <= PAGE_SIZE, "per-block statistics are kept one block per lane"
    pages_t = jnp.transpose(cache, (0, 2, 3, 4, 1))
    q_b = jnp.broadcast_to(_as_dot_operand(q)[..., None], (n, heads, head_dim, PAGE_SIZE))
    slopes = jnp.broadcast_to(jnp.asarray(_alibi_slopes(heads), F32)[:, None, None], (heads, 1, PAGE_SIZE))
    row4 = lambda i, j, pt: (i, 0, 0, 0)
    row5 = lambda i, j, pt: (i, 0, 0, 0, 0)
    pages_per_step = per_blk * DECODE_BLOCKS

    def page_spec(k):
        return pl.BlockSpec((1, 2, heads, head_dim, PAGE_SIZE),
                            lambda i, j, pt: (pt[i, pages_per_step * j + k], 0, 0, 0, 0))

    grid_spec = pltpu.PrefetchScalarGridSpec(
        num_scalar_prefetch=1,
        grid=(n, n_pages // pages_per_step),
        in_specs=[pl.BlockSpec((1, heads, head_dim, PAGE_SIZE), row4),
                  pl.BlockSpec((1, 2, heads, head_dim, 1), row5),
                  pl.BlockSpec(slopes.shape, lambda i, j, pt: (0, 0, 0))]
        + [page_spec(k) for k in range(pages_per_step)],
        out_specs=pl.BlockSpec((1, heads, head_dim, 1), row4),
        scratch_shapes=[pltpu.VMEM((heads, 1, PAGE_SIZE), F32),
                        pltpu.VMEM((heads, 1, PAGE_SIZE), F32),
                        pltpu.VMEM((heads, 1, PAGE_SIZE), F32),
                        pltpu.VMEM((heads, head_dim, PAGE_SIZE), F32)])
    o = pl.pallas_call(
        functools.partial(_moba_decode_kernel, past_len=n_pages * PAGE_SIZE),
        grid_spec=grid_spec,
        out_shape=jax.ShapeDtypeStruct((n, heads, head_dim, 1), F32),
        compiler_params=_params("arbitrary", "arbitrary"),
        name="moba_decode",
    )(page_table, q_b, kv_new[..., None], slopes, *([pages_t] * pages_per_step))
    return o.reshape(n, heads, head_dim)


def _moba_layer(yp, ys, cache, page_table, g_mix, w_qkv, q_gain, k_gain, w_o):
    b, s, d = yp.shape
    n = ys.shape[0]
    width = w_qkv.shape[1] // 3
    head_dim = width // MOBA_HEADS
    w16 = w_qkv.astype(BF16)
    wo16 = w_o.astype(BF16)
    gains = jnp.stack([_tiled_gain(q_gain, MOBA_HEADS), _tiled_gain(k_gain, MOBA_HEADS), jnp.ones((width,), F32)])
    gains = gains.reshape(3, 1, width)
    q32, kv32, k16, vt16, km = _moba_proj(yp, g_mix, w16, gains, prompt=True)
    yp_new = _moba_attn(q32, k16, vt16, km.reshape(b, s // MOBA_BLOCK, width), wo16, yp)
    rows_p = kv32.reshape(b, s, 2, MOBA_HEADS, head_dim)
    qs32, kvs32 = _moba_proj(ys[None], g_mix, w16, gains, prompt=False)
    att_s = _moba_decode(qs32.reshape(n, MOBA_HEADS, head_dim), kvs32.reshape(n, 2, MOBA_HEADS, head_dim),
                         cache, page_table)
    ys_new = _out_proj(att_s.reshape(n, width), wo16, ys)
    rows_s = kvs32.reshape(n, 1, 2, MOBA_HEADS, head_dim)
    return yp_new, ys_new, rows_p, rows_s


def kernel(x_prompt, x_sample, state_pool, cache_dil0, cache_dil1, cache_dil2, state_conv, cache_moba, page_table,
           norm_mix, norm_ffn, pool_w, pool_scale, dil_w_qkv, dil_q_gain, dil_k_gain, dil_w_o,
           moba_w_qkv, moba_q_gain, moba_k_gain, moba_w_o, ffn_w_in, ffn_conv_w, ffn_conv_b, ffn_w_out):
    assert x_sample.shape[1] == 1, "the sample group decodes one token per sequence"
    depth = norm_mix.shape[0]
    yp, ys = x_prompt, x_sample[:, 0]
    pool_p, pool_s, moba_p, moba_s, conv_p, conv_s = [], [], [], [], [], []
    dil_p = [[] for _ in DIL_PATTERNS]
    dil_s = [[] for _ in DIL_PATTERNS]
    for layer in range(depth):
        kind, j = layer % N_MIXERS, layer // N_MIXERS
        g_mix = norm_mix[layer][None]
        if kind == 0:
            w16 = pool_w[j].astype(BF16)
            scale = pool_scale[j][None]
            yp, tail = _pool_prompt(yp, g_mix, w16, scale)
            ys, hs = _pool_sample(ys, state_pool[j].transpose(1, 0, 2), g_mix, w16, scale)
            pool_p.append(tail[:, POOL_CARRY - POOL_HIST:])
            pool_s.append(jnp.concatenate([state_pool[j][:, 1:], hs[:, None]], axis=1))
        elif kind == 1:
            caches = [c[j] for c in (cache_dil0, cache_dil1, cache_dil2)]
            yp, ys, rows_p, rows_s = _dil_layer(yp, ys, caches, g_mix, dil_w_qkv[j], dil_q_gain[j], dil_k_gain[j],
                                                dil_w_o[j])
            for g in range(len(DIL_PATTERNS)):
                dil_p[g].append(rows_p[g])
                dil_s[g].append(rows_s[g])
        else:
            yp, ys, rows_p, rows_s = _moba_layer(yp, ys, cache_moba[j], page_table, g_mix, moba_w_qkv[j],
                                                 moba_q_gain[j], moba_k_gain[j], moba_w_o[j])
            moba_p.append(rows_p)
            moba_s.append(rows_s)
        g_ffn = norm_ffn[layer][None]
        win16 = ffn_w_in[layer].astype(BF16)
        wout16 = ffn_w_out[layer].astype(BF16)
        conv_w, conv_b = ffn_conv_w[layer], ffn_conv_b[layer][None]
        yp, tail = _ffn_prompt(yp, g_ffn, win16, conv_w, conv_b, wout16)
        ys, ua, ug = _ffn_sample(ys, g_ffn, win16, state_conv[layer].transpose(1, 0, 2), conv_w, conv_b, wout16)
        conv_p.append(tail[:, CONV_CARRY - CONV_HIST:])
        u_new = jnp.concatenate([ua, ug], axis=-1)[:, None]
        conv_s.append(jnp.concatenate([state_conv[layer][:, 1:], u_new], axis=1))
    stack = lambda xs: jnp.stack(xs, axis=0)
    return (yp, ys[:, None], stack(pool_p), stack(pool_s),
            stack(dil_p[0]), stack(dil_p[1]), stack(dil_p[2]),
            stack(dil_s[0]), stack(dil_s[1]), stack(dil_s[2]),
            stack(moba_p), stack(moba_s), stack(conv_p), stack(conv_s))
```
